```python
import jax, jax.numpy as jnp
from jax import lax
import numpy as np

D_MODEL = 1024
BATCH = 4
SEQ = 4096
DEPTH = 4

CHUNK = 64
N_MIXERS = 4
MIX_WIDTH = D_MODEL
GROUP_WIDTH = MIX_WIDTH // N_MIXERS
N_HEADS_PER_MIXER = 4
HEAD_DIM = GROUP_WIDTH // N_HEADS_PER_MIXER
CONFORMER_KERNEL = 31
SHORT_CONV_KERNEL = 3
POOL_WINDOWS = (2, 4, 8, 16)
SGU_BLOCK = 128
N_IN_SLICES = 12
IN_WIDTH = N_IN_SLICES * GROUP_WIDTH
LN_EPS = 1e-5

kernel_name = "hybrid_conv_pool_sgu_deepnorm_trunk"


def layer_norm(x, g, b):
    xf = x.astype(jnp.float32)
    mu = jnp.mean(xf, axis=-1, keepdims=True)
    var = jnp.mean(jnp.square(xf - mu), axis=-1, keepdims=True)
    y = (xf - mu) * lax.rsqrt(var + LN_EPS)
    return (y * g + b).astype(x.dtype)


def causal_dwconv(x, w):
    k, c = w.shape
    return lax.conv_general_dilated(
        x, w[:, None, :].astype(x.dtype), window_strides=(1,), padding=[(k - 1, 0)],
        dimension_numbers=("NWC", "WIO", "NWC"), feature_group_count=c)


def multi_scale_pool(h):
    bsz, s, _ = h.shape
    hg = h.reshape(bsz, s, len(POOL_WINDOWS), HEAD_DIM).astype(jnp.float32)
    cs = jnp.cumsum(hg, axis=1)
    pos1 = jnp.arange(1, s + 1)
    means = []
    for g, w in enumerate(POOL_WINDOWS):
        c = cs[:, :, g]
        prev = jnp.pad(c[:, : s - w], ((0, 0), (w, 0), (0, 0)))
        cnt = jnp.minimum(pos1, w).astype(jnp.float32)[None, :, None]
        means.append((c - prev) / cnt)
    mean = jnp.stack(means, axis=2)
    return (mean - hg).astype(h.dtype)


def sgu_mask():
    idx = jnp.arange(SGU_BLOCK) // CHUNK
    return (idx[None, :] <= idx[:, None])


def hybrid_layer(x, ln_g, ln_b, w_in, b_in, conv_a_w, conv_a_b, norm_a_g, norm_a_b,
                 conv_b_w, pool_w, pool_scale, sgu_ln_g, sgu_ln_b, sgu_w, sgu_bias,
                 w_out, b_out):
    bsz, s, _ = x.shape
    alpha = float((2.0 * DEPTH) ** 0.25)
    h = jnp.einsum("bsd,de->bse", x, w_in) + b_in
    (a_val, a_glu, a_z, b_b, b_c, b_h, b_z, c_h, c_z, d_u, d_v, d_z) = jnp.split(h, N_IN_SLICES, axis=-1)

    a = a_val * jax.nn.sigmoid(a_glu)
    a = causal_dwconv(a, conv_a_w) + conv_a_b
    a = layer_norm(a.reshape(bsz, s, N_HEADS_PER_MIXER, HEAD_DIM),
                   norm_a_g.reshape(N_HEADS_PER_MIXER, HEAD_DIM),
                   norm_a_b.reshape(N_HEADS_PER_MIXER, HEAD_DIM)).reshape(bsz, s, GROUP_WIDTH)
    y_a = jax.nn.silu(a) * jax.nn.silu(a_z)

    y_b = b_b * causal_dwconv(b_c * b_h, conv_b_w) * jax.nn.silu(b_z)

    pooled = multi_scale_pool(c_h)
    y_c = jnp.einsum("bsgc,gcd->bsgd", pooled, pool_w).reshape(bsz, s, GROUP_WIDTH)
    y_c = y_c * pool_scale * jax.nn.silu(c_z)

    v = layer_norm(d_v, sgu_ln_g, sgu_ln_b)
    vb = v.reshape(bsz, s // SGU_BLOCK, SGU_BLOCK, N_HEADS_PER_MIXER, HEAD_DIM)
    w_s = jnp.where(sgu_mask()[None], sgu_w, jnp.zeros_like(sgu_w))
    sp = jnp.einsum("hij,bnjhc->bnihc", w_s, vb) + sgu_bias.T[None, None, :, :, None]
    y_d = d_u * sp.reshape(bsz, s, GROUP_WIDTH) * jax.nn.silu(d_z)

    mix = jnp.concatenate([y_a, y_b, y_c, y_d], axis=-1)
    out = jnp.einsum("bse,ed->bsd", mix, w_out) + b_out
    return layer_norm(alpha * x + out, ln_g, ln_b)


def setup_inputs(seed: int = 0) -> dict:
    key = jax.random.key(seed)
    ks = jax.random.split(key, 20)
    f32 = jnp.float32
    L = DEPTH
    beta = (8.0 * DEPTH) ** -0.25
    nrm = lambda k, shape, sc: jax.random.normal(k, shape, f32) * sc
    return {
        "x": jax.random.normal(ks[0], (BATCH, SEQ, D_MODEL), f32),
        "ln_g": 1.0 + nrm(ks[1], (L, D_MODEL), 0.05),
        "ln_b": nrm(ks[2], (L, D_MODEL), 0.02),
        "w_in": nrm(ks[3], (L, D_MODEL, IN_WIDTH), D_MODEL ** -0.5),
        "b_in": nrm(ks[4], (L, IN_WIDTH), 0.02),
        "conv_a_w": nrm(ks[5], (L, CONFORMER_KERNEL, GROUP_WIDTH), CONFORMER_KERNEL ** -0.5),
        "conv_a_b": nrm(ks[6], (L, GROUP_WIDTH), 0.02),
        "norm_a_g": 1.0 + nrm(ks[7], (L, GROUP_WIDTH), 0.05),
        "norm_a_b": nrm(ks[8], (L, GROUP_WIDTH), 0.02),
        "conv_b_w": nrm(ks[9], (L, SHORT_CONV_KERNEL, GROUP_WIDTH), SHORT_CONV_KERNEL ** -0.5),
        "pool_w": nrm(ks[10], (L, len(POOL_WINDOWS), HEAD_DIM, HEAD_DIM), HEAD_DIM ** -0.5),
        "pool_scale": 1.0 + nrm(ks[11], (L, GROUP_WIDTH), 0.1),
        "sgu_ln_g": 1.0 + nrm(ks[12], (L, GROUP_WIDTH), 0.05),
        "sgu_ln_b": nrm(ks[13], (L, GROUP_WIDTH), 0.02),
        "sgu_w": nrm(ks[14], (L, N_HEADS_PER_MIXER, SGU_BLOCK, SGU_BLOCK), SGU_BLOCK ** -0.5),
        "sgu_bias": 1.0 + nrm(ks[15], (L, N_HEADS_PER_MIXER, SGU_BLOCK), 0.1),
        "w_out": nrm(ks[16], (L, MIX_WIDTH, D_MODEL), beta * MIX_WIDTH ** -0.5),
        "b_out": nrm(ks[17], (L, D_MODEL), 0.01),
    }


def reference(x, ln_g, ln_b, w_in, b_in, conv_a_w, conv_a_b, norm_a_g, norm_a_b,
              conv_b_w, pool_w, pool_scale, sgu_ln_g, sgu_ln_b, sgu_w, sgu_bias,
              w_out, b_out):
    h = x
    for l in range(DEPTH):
        h = hybrid_layer(h, ln_g[l], ln_b[l], w_in[l], b_in[l], conv_a_w[l], conv_a_b[l],
                         norm_a_g[l], norm_a_b[l], conv_b_w[l], pool_w[l], pool_scale[l],
                         sgu_ln_g[l], sgu_ln_b[l], sgu_w[l], sgu_bias[l], w_out[l], b_out[l])
    return h
```

```python
import functools

import jax
import jax.numpy as jnp
import numpy as np
from jax.experimental import pallas as pl
from jax.experimental.pallas import tpu as pltpu

D_MODEL = 1024
DEPTH = 4
GROUP = 256
HEAD = 64
CONV_A_K = 31
CONV_B_K = 3
POOL_WINDOWS = (2, 4, 8, 16)
SGU_BLOCK = 128
CHUNK = 64
LN_EPS = 1e-5
ALPHA = float((2.0 * DEPTH) ** 0.25)

LANES = 128
SUBLANES = 8
TILE_ROWS = 512
HALO_A = 32
HALO_B = 8
HALO_C = 16
PAD_C = 8
VMEM_LIMIT_BYTES = 56 * 1024 * 1024

F32 = jnp.float32
BF16 = jnp.bfloat16


def _silu(v):
    return v * jax.nn.sigmoid(v)


def _dot(a, b):
    return jnp.dot(a, b, preferred_element_type=F32)


def _layer_kernel(x_ref, win_ref, bin_ref, caw_ref, cab_ref, nag_ref, nab_ref, cbw_ref,
                  pbd_ref, psc_ref, corr_ref, sg_ref, sb_ref, swb_ref, sbias_ref,
                  wout_ref, bout_ref, lng_ref, lnb_ref, phead_ref,
                  o_ref,
                  abuf, bbuf, cbuf, s2buf, s4buf, s8buf, tbuf, mix,
                  *, tiles_per_seq):
    rows = TILE_ROWS
    pid = pl.program_id(0)
    dz = jnp.minimum(pid, 0)
    first = (pid % tiles_per_seq) == 0

    @pl.when(first)
    def _zero_history():
        abuf[:, 0:HALO_A, :] = jnp.zeros((2, HALO_A, LANES), F32)
        bbuf[:, 0:HALO_B, :] = jnp.zeros((2, HALO_B, LANES), F32)
        cbuf[:, 0:PAD_C + HALO_C, :] = jnp.zeros((2, PAD_C + HALO_C, LANES), F32)
        s2buf[:, 0:PAD_C, :] = jnp.zeros((2, PAD_C, LANES), F32)

    xb = x_ref[...].astype(BF16)

    def proj(i):
        lo, hi = GROUP * i, GROUP * (i + 1)
        return _dot(xb, win_ref[:, lo:hi]) + bin_ref[:, lo:hi]

    def shifted(buf, slab, start, n):
        return buf[slab, pl.ds(dz + start, n), :]

    a = proj(0) * jax.nn.sigmoid(proj(1))
    for s in range(2):
        abuf[s, HALO_A:HALO_A + rows, :] = a[:, LANES * s:LANES * (s + 1)]
    for s in range(2):
        lo, hi = LANES * s, LANES * (s + 1)
        wk = [jnp.broadcast_to(caw_ref[k:k + 1, lo:hi], (SUBLANES, LANES)) for k in range(CONV_A_K)]
        bias = jnp.broadcast_to(cab_ref[:, lo:hi], (SUBLANES, LANES))
        for c in range(rows // SUBLANES):
            base = HALO_A + SUBLANES * c - (CONV_A_K - 1)
            acc = bias
            for k in range(CONV_A_K):
                acc = acc + shifted(abuf, s, base + k, SUBLANES) * wk[k]
            tbuf[SUBLANES * c:SUBLANES * (c + 1), lo:hi] = acc
    t = tbuf[...]
    mu = _dot(t.astype(BF16), phead_ref[...])
    d = t - mu
    var = _dot((d * d).astype(BF16), phead_ref[...])
    an = d * jax.lax.rsqrt(var + LN_EPS) * nag_ref[...] + nab_ref[...]
    y_a = _silu(an) * _silu(proj(2))
    mix[:, 0:GROUP] = y_a.astype(BF16)

    bc = proj(4) * proj(5)
    for s in range(2):
        bbuf[s, HALO_B:HALO_B + rows, :] = bc[:, LANES * s:LANES * (s + 1)]
    conv_b = []
    for s in range(2):
        lo, hi = LANES * s, LANES * (s + 1)
        acc = shifted(bbuf, s, HALO_B, rows) * cbw_ref[CONV_B_K - 1:CONV_B_K, lo:hi]
        for k in range(CONV_B_K - 1):
            acc = acc + shifted(bbuf, s, HALO_B - (CONV_B_K - 1) + k, rows) * cbw_ref[k:k + 1, lo:hi]
        conv_b.append(acc)
    y_b = proj(3) * jnp.concatenate(conv_b, axis=-1) * _silu(proj(6))
    mix[:, GROUP:2 * GROUP] = y_b.astype(BF16)

    ch = proj(7)
    c0 = PAD_C + HALO_C
    for s in range(2):
        cbuf[s, c0:c0 + rows, :] = ch[:, LANES * s:LANES * (s + 1)]
    lane = jax.lax.broadcasted_iota(jnp.int32, (1, LANES), 1)
    low_half = lane < HEAD
    n2 = rows + HALO_C
    s2buf[:, PAD_C:PAD_C + n2, :] = jnp.stack(
        [shifted(cbuf, s, PAD_C, n2) + shifted(cbuf, s, PAD_C - 1, n2) for s in range(2)])
    s4buf[:, PAD_C:PAD_C + n2, :] = jnp.stack(
        [shifted(s2buf, s, PAD_C, n2) + shifted(s2buf, s, PAD_C - 2, n2) for s in range(2)])
    n8 = rows + 8
    s8buf[1, c0 - 8:c0 - 8 + n8, :] = shifted(s4buf, 1, c0 - 8, n8) + shifted(s4buf, 1, c0 - 12, n8)
    s16 = shifted(s8buf, 1, c0, rows) + shifted(s8buf, 1, c0 - 8, rows)
    mean0 = jnp.where(low_half, s2buf[0, c0:c0 + rows, :] * 0.5, s4buf[0, c0:c0 + rows, :] * 0.25)
    mean1 = jnp.where(low_half, s8buf[1, c0:c0 + rows, :] * 0.125, s16 * 0.0625)
    tbuf[:, 0:LANES] = mean0
    tbuf[:, LANES:GROUP] = mean1

    @pl.when(first)
    def _short_history_counts():
        tbuf[0:HALO_C, :] = tbuf[0:HALO_C, :] * corr_ref[...]

    pooled = tbuf[...] - ch
    y_c = _dot(pooled.astype(BF16), pbd_ref[...]) * psc_ref[...] * _silu(proj(8))
    mix[:, 2 * GROUP:3 * GROUP] = y_c.astype(BF16)

    dv = proj(10)
    vmu = jnp.mean(dv, axis=-1, keepdims=True)
    vd = dv - vmu
    vvar = jnp.mean(vd * vd, axis=-1, keepdims=True)
    v = (vd * jax.lax.rsqrt(vvar + LN_EPS) * sg_ref[...] + sb_ref[...]).astype(BF16)
    wi = jax.lax.broadcasted_iota(jnp.int32, (SGU_BLOCK, 4 * SGU_BLOCK), 0)
    wj = jax.lax.broadcasted_iota(jnp.int32, (SGU_BLOCK, 4 * SGU_BLOCK), 1) % SGU_BLOCK
    w_s = jnp.where(wj // CHUNK <= wi // CHUNK, swb_ref[...], 0.0).astype(BF16)
    head_of_lane = jax.lax.broadcasted_iota(jnp.int32, (1, GROUP), 1) // HEAD
    sp = []
    for n in range(rows // SGU_BLOCK):
        blk = v[SGU_BLOCK * n:SGU_BLOCK * (n + 1), :]
        vbig = jnp.concatenate(
            [jnp.where(head_of_lane == h, blk, jnp.zeros_like(blk)) for h in range(4)], axis=0)
        sp.append(_dot(w_s, vbig) + sbias_ref[...])
    y_d = proj(9) * jnp.concatenate(sp, axis=0) * _silu(proj(11))
    mix[:, 3 * GROUP:4 * GROUP] = y_d.astype(BF16)

    out = _dot(mix[...], wout_ref[...]) + bout_ref[...]
    r = ALPHA * x_ref[...] + out
    rmu = jnp.mean(r, axis=-1, keepdims=True)
    rd = r - rmu
    rvar = jnp.mean(rd * rd, axis=-1, keepdims=True)
    o_ref[...] = rd * jax.lax.rsqrt(rvar + LN_EPS) * lng_ref[...] + lnb_ref[...]

    abuf[:, 0:HALO_A, :] = abuf[:, rows:rows + HALO_A, :]
    bbuf[:, 0:HALO_B, :] = bbuf[:, rows:rows + HALO_B, :]
    cbuf[:, PAD_C:PAD_C + HALO_C, :] = cbuf[:, rows + PAD_C:rows + PAD_C + HALO_C, :]


def _full(shape):
    return pl.BlockSpec(shape, lambda i: (0,) * len(shape))


def _layer(x2, seq_len, params):
    n = x2.shape[0]
    assert seq_len % TILE_ROWS == 0 and TILE_ROWS % SGU_BLOCK == 0
    in_specs = [pl.BlockSpec((TILE_ROWS, D_MODEL), lambda i: (i, 0))]
    in_specs += [_full(p.shape) for p in params]
    hist = lambda halo: pltpu.VMEM((2, halo + TILE_ROWS, LANES), F32)
    return pl.pallas_call(
        functools.partial(_layer_kernel, tiles_per_seq=seq_len // TILE_ROWS),
        grid=(n // TILE_ROWS,),
        in_specs=in_specs,
        out_specs=pl.BlockSpec((TILE_ROWS, D_MODEL), lambda i: (i, 0)),
        out_shape=jax.ShapeDtypeStruct((n, D_MODEL), F32),
        scratch_shapes=[
            hist(HALO_A),
            hist(HALO_B),
            hist(PAD_C + HALO_C),
            hist(PAD_C + HALO_C),
            hist(PAD_C + HALO_C),
            hist(PAD_C + HALO_C),
            pltpu.VMEM((TILE_ROWS, GROUP), F32),
            pltpu.VMEM((TILE_ROWS, D_MODEL), BF16),
        ],
        compiler_params=pltpu.CompilerParams(
            dimension_semantics=("arbitrary",), vmem_limit_bytes=VMEM_LIMIT_BYTES),
        name="hybrid_layer",
    )(x2, *params)


def _constants():
    head_id = np.arange(GROUP) // HEAD
    phead = (head_id[:, None] == head_id[None, :]).astype(np.float32) / HEAD
    win = np.asarray(POOL_WINDOWS, np.float32)[head_id]
    pos1 = np.arange(1, HALO_C + 1, dtype=np.float32)[:, None]
    corr = win[None, :] / np.minimum(pos1, win[None, :])
    return jnp.asarray(phead, BF16), jnp.asarray(corr, F32)


def kernel(x, ln_g, ln_b, w_in, b_in, conv_a_w, conv_a_b, norm_a_g, norm_a_b, conv_b_w, pool_w, pool_scale, sgu_ln_g, sgu_ln_b, sgu_w, sgu_bias, w_out, b_out):
    bsz, seq_len, d = x.shape
    phead, corr = _constants()
    row = lambda v: v.reshape(1, -1)
    h = x.reshape(bsz * seq_len, d)
    for l in range(DEPTH):
        pool_bd = jax.scipy.linalg.block_diag(*[pool_w[l, g] for g in range(len(POOL_WINDOWS))])
        sgu_wide = jnp.transpose(sgu_w[l], (1, 0, 2)).reshape(SGU_BLOCK, 4 * SGU_BLOCK)
        sgu_bias_tbl = jnp.repeat(sgu_bias[l].T, HEAD, axis=1)
        params = (
            w_in[l].astype(BF16), row(b_in[l]),
            conv_a_w[l], row(conv_a_b[l]), row(norm_a_g[l]), row(norm_a_b[l]),
            conv_b_w[l],
            pool_bd.astype(BF16), row(pool_scale[l]), corr,
            row(sgu_ln_g[l]), row(sgu_ln_b[l]), sgu_wide, sgu_bias_tbl,
            w_out[l].astype(BF16), row(b_out[l]), row(ln_g[l]), row(ln_b[l]),
            phead,
        )
        h = _layer(h, seq_len, params)
    return h.reshape(bsz, seq_len, d)
```

```python
import functools

import jax
import jax.numpy as jnp
import numpy as np
from jax.experimental import pallas as pl
from jax.experimental.pallas import tpu as pltpu

D_MODEL = 1024
DEPTH = 4
GROUP = 256
HEAD = 64
CONV_A_K = 31
CONV_B_K = 3
POOL_WINDOWS = (2, 4, 8, 16)
SGU_BLOCK = 128
CHUNK = 64
LN_EPS = 1e-5
ALPHA = float((2.0 * DEPTH) ** 0.25)

LANES = 128
SUBLANES = 8
TILE_ROWS = 512
HALO_A = 32
HALO_B = 8
HALO_C = 16
PAD_C = 8
N_KEPT = 5
CONV_A_UNROLL = 4
VMEM_LIMIT_BYTES = 56 * 1024 * 1024

F32 = jnp.float32
BF16 = jnp.bfloat16


def _silu(v):
    return v * jax.nn.sigmoid(v)


def _dot(a, b):
    return jnp.dot(a, b, preferred_element_type=F32)


def _layer_kernel(x_ref, xprev_ref, win_ref, bin_ref, caw_ref, cab_ref, nag_ref, nab_ref, cbw_ref,
                  pbd_ref, psc_ref, corr_ref, sg_ref, sb_ref, swb_ref, sbias_ref,
                  wout_ref, bout_ref, lng_ref, lnb_ref, phead_ref,
                  o_ref,
                  abuf, bbuf, cbuf, kept, s2buf, s4buf, s8buf, tbuf, mix,
                  *, tiles_per_seq):
    rows = TILE_ROWS
    pid = pl.program_id(0)
    dz = jnp.minimum(pid, 0)
    cur = pid % 2
    prev = 1 - cur
    c0 = PAD_C + HALO_C

    @pl.when(pid == 0)
    def _init():
        abuf[1] = jnp.zeros(abuf.shape[1:], F32)
        bbuf[1] = jnp.zeros(bbuf.shape[1:], F32)
        cbuf[...] = jnp.zeros(cbuf.shape, F32)
        kept[1] = jnp.zeros(kept.shape[1:], F32)
        s2buf[:, 0:PAD_C, :] = jnp.zeros((2, PAD_C, LANES), F32)

    def shifted(ref, start, n):
        return ref[pl.ds(dz + start, n), :]

    xb = x_ref[...].astype(BF16)

    def proj(i):
        lo, hi = GROUP * i, GROUP * (i + 1)
        return _dot(xb, win_ref[:, lo:hi]) + bin_ref[:, lo:hi]

    def put_slabs(buf, halo, val):
        for s in range(2):
            buf[cur, s, halo:halo + rows, :] = val[:, LANES * s:LANES * (s + 1)]

    def proj_a():
        put_slabs(abuf, HALO_A, proj(0) * jax.nn.sigmoid(proj(1)))

    def proj_a_gate():
        kept[cur, 0] = _silu(proj(2))

    def proj_b_gate():
        kept[cur, 1] = proj(3) * _silu(proj(6))

    def proj_b():
        put_slabs(bbuf, HALO_B, proj(4) * proj(5))

    def proj_c():
        put_slabs(cbuf, c0, proj(7))

    def proj_c_gate():
        kept[cur, 2] = _silu(proj(8)) * psc_ref[...]

    def proj_d_gate():
        kept[cur, 3] = proj(9) * _silu(proj(11))

    def proj_d():
        kept[cur, 4] = proj(10)

    first_prev = ((pid - 1) % tiles_per_seq) == 0

    def conv_a(s):
        lo, hi = LANES * s, LANES * (s + 1)
        src = abuf.at[prev, s]
        bias = jnp.broadcast_to(cab_ref[:, lo:hi], (SUBLANES, LANES))
        for g in range(rows // (SUBLANES * CONV_A_UNROLL)):
            base = HALO_A + SUBLANES * CONV_A_UNROLL * g - (CONV_A_K - 1)
            accs = [bias] * CONV_A_UNROLL
            for k in range(CONV_A_K):
                wk = jnp.broadcast_to(caw_ref[k:k + 1, lo:hi], (SUBLANES, LANES))
                for j in range(CONV_A_UNROLL):
                    accs[j] = accs[j] + shifted(src, base + SUBLANES * j + k, SUBLANES) * wk
            for j in range(CONV_A_UNROLL):
                r0 = SUBLANES * (CONV_A_UNROLL * g + j)
                tbuf[r0:r0 + SUBLANES, lo:hi] = accs[j]

    def norm_a():
        t = tbuf[...]
        mu = _dot(t.astype(BF16), phead_ref[...])
        d = t - mu
        var = _dot((d * d).astype(BF16), phead_ref[...])
        an = d * jax.lax.rsqrt(var + LN_EPS) * nag_ref[...] + nab_ref[...]
        mix[:, 0:GROUP] = (_silu(an) * kept[prev, 0]).astype(BF16)

    def branch_b():
        conv_b = []
        for s in range(2):
            lo, hi = LANES * s, LANES * (s + 1)
            src = bbuf.at[prev, s]
            acc = shifted(src, HALO_B, rows) * cbw_ref[CONV_B_K - 1:CONV_B_K, lo:hi]
            for k in range(CONV_B_K - 1):
                acc = acc + shifted(src, HALO_B - (CONV_B_K - 1) + k, rows) * cbw_ref[k:k + 1, lo:hi]
            conv_b.append(acc)
        mix[:, GROUP:2 * GROUP] = (jnp.concatenate(conv_b, axis=-1) * kept[prev, 1]).astype(BF16)

    def branch_c():
        lane = jax.lax.broadcasted_iota(jnp.int32, (1, LANES), 1)
        low_half = lane < HEAD
        csrc = [cbuf.at[prev, s] for s in range(2)]
        n2 = rows + HALO_C
        for s in range(2):
            s2buf[s, PAD_C:PAD_C + n2, :] = shifted(csrc[s], PAD_C, n2) + shifted(csrc[s], PAD_C - 1, n2)
        for s in range(2):
            s4buf[s, PAD_C:PAD_C + n2, :] = (
                shifted(s2buf.at[s], PAD_C, n2) + shifted(s2buf.at[s], PAD_C - 2, n2))
        n8 = rows + 8
        s8buf[c0 - 8:c0 - 8 + n8, :] = shifted(s4buf.at[1], c0 - 8, n8) + shifted(s4buf.at[1], c0 - 12, n8)
        s16 = shifted(s8buf, c0, rows) + shifted(s8buf, c0 - 8, rows)
        mean0 = jnp.where(low_half, s2buf[0, c0:c0 + rows, :] * 0.5, s4buf[0, c0:c0 + rows, :] * 0.25)
        mean1 = jnp.where(low_half, s8buf[c0:c0 + rows, :] * 0.125, s16 * 0.0625)
        tbuf[:, 0:LANES] = mean0
        tbuf[:, LANES:GROUP] = mean1
        tbuf[0:HALO_C, :] = tbuf[0:HALO_C, :] * jnp.where(first_prev, corr_ref[...], 1.0)
        ch = jnp.concatenate([shifted(csrc[s], c0, rows) for s in range(2)], axis=-1)
        pooled = tbuf[...] - ch
        mix[:, 2 * GROUP:3 * GROUP] = (_dot(pooled.astype(BF16), pbd_ref[...]) * kept[prev, 2]).astype(BF16)

    def branch_d():
        dv = kept[prev, 4]
        vmu = jnp.mean(dv, axis=-1, keepdims=True)
        vd = dv - vmu
        vvar = jnp.mean(vd * vd, axis=-1, keepdims=True)
        v = (vd * jax.lax.rsqrt(vvar + LN_EPS) * sg_ref[...] + sb_ref[...]).astype(BF16)
        wi = jax.lax.broadcasted_iota(jnp.int32, (SGU_BLOCK, 4 * SGU_BLOCK), 0)
        wj = jax.lax.broadcasted_iota(jnp.int32, (SGU_BLOCK, 4 * SGU_BLOCK), 1) % SGU_BLOCK
        w_s = jnp.where(wj // CHUNK <= wi // CHUNK, swb_ref[...], 0.0).astype(BF16)
        head_of_lane = jax.lax.broadcasted_iota(jnp.int32, (1, GROUP), 1) // HEAD
        sp = []
        for n in range(rows // SGU_BLOCK):
            blk = v[SGU_BLOCK * n:SGU_BLOCK * (n + 1), :]
            vbig = jnp.concatenate(
                [jnp.where(head_of_lane == h, blk, jnp.zeros_like(blk)) for h in range(4)], axis=0)
            sp.append(_dot(w_s, vbig) + sbias_ref[...])
        mix[:, 3 * GROUP:4 * GROUP] = (jnp.concatenate(sp, axis=0) * kept[prev, 3]).astype(BF16)

    def out_proj():
        return _dot(mix[...], wout_ref[...]) + bout_ref[...]

    def post_norm(out):
        r = ALPHA * xprev_ref[...] + out
        rmu = jnp.mean(r, axis=-1, keepdims=True)
        rd = r - rmu
        rvar = jnp.mean(rd * rd, axis=-1, keepdims=True)
        o_ref[...] = rd * jax.lax.rsqrt(rvar + LN_EPS) * lng_ref[...] + lnb_ref[...]

    conv_a(0)
    proj_a()
    conv_a(1)
    proj_a_gate()
    proj_b_gate()
    norm_a()
    proj_b()
    branch_b()
    branch_c()
    proj_c()
    branch_d()
    proj_c_gate()
    out = out_proj()
    proj_d_gate()
    post_norm(out)
    proj_d()

    fresh = (pid % tiles_per_seq) == 0
    abuf[cur, :, 0:HALO_A, :] = jnp.where(fresh, 0.0, abuf[prev, :, rows:rows + HALO_A, :])
    bbuf[cur, :, 0:HALO_B, :] = jnp.where(fresh, 0.0, bbuf[prev, :, rows:rows + HALO_B, :])
    cbuf[cur, :, PAD_C:c0, :] = jnp.where(fresh, 0.0, cbuf[prev, :, rows + PAD_C:rows + c0, :])


def _full(shape):
    return pl.BlockSpec(shape, lambda i: (0,) * len(shape))


def _layer(x2, seq_len, params):
    n = x2.shape[0]
    assert seq_len % TILE_ROWS == 0 and TILE_ROWS % SGU_BLOCK == 0
    n_tiles = n // TILE_ROWS
    tile = (TILE_ROWS, D_MODEL)
    in_specs = [pl.BlockSpec(tile, lambda i: (jnp.minimum(i, n_tiles - 1), 0)),
                pl.BlockSpec(tile, lambda i: (jnp.maximum(i - 1, 0), 0))]
    in_specs += [_full(p.shape) for p in params]
    hist = lambda halo: pltpu.VMEM((2, 2, halo + TILE_ROWS, LANES), F32)
    work = lambda *lead: pltpu.VMEM((*lead, PAD_C + HALO_C + TILE_ROWS, LANES), F32)
    return pl.pallas_call(
        functools.partial(_layer_kernel, tiles_per_seq=seq_len // TILE_ROWS),
        grid=(n_tiles + 1,),
        in_specs=in_specs,
        out_specs=pl.BlockSpec(tile, lambda i: (jnp.maximum(i - 1, 0), 0)),
        out_shape=jax.ShapeDtypeStruct((n, D_MODEL), F32),
        scratch_shapes=[
            hist(HALO_A),
            hist(HALO_B),
            hist(PAD_C + HALO_C),
            pltpu.VMEM((2, N_KEPT, TILE_ROWS, GROUP), F32),
            work(2),
            work(2),
            work(),
            pltpu.VMEM((TILE_ROWS, GROUP), F32),
            pltpu.VMEM((TILE_ROWS, D_MODEL), BF16),
        ],
        compiler_params=pltpu.CompilerParams(
            dimension_semantics=("arbitrary",), vmem_limit_bytes=VMEM_LIMIT_BYTES),
        name="hybrid_layer",
    )(x2, x2, *params)


def _constants():
    head_id = np.arange(GROUP) // HEAD
    phead = (head_id[:, None] == head_id[None, :]).astype(np.float32) / HEAD
    win = np.asarray(POOL_WINDOWS, np.float32)[head_id]
    pos1 = np.arange(1, HALO_C + 1, dtype=np.float32)[:, None]
    corr = win[None, :] / np.minimum(pos1, win[None, :])
    return jnp.asarray(phead, BF16), jnp.asarray(corr, F32)


def kernel(x, ln_g, ln_b, w_in, b_in, conv_a_w, conv_a_b, norm_a_g, norm_a_b, conv_b_w, pool_w, pool_scale, sgu_ln_g, sgu_ln_b, sgu_w, sgu_bias, w_out, b_out):
    bsz, seq_len, d = x.shape
    phead, corr = _constants()
    row = lambda v: v.reshape(1, -1)
    h = x.reshape(bsz * seq_len, d)
    for l in range(DEPTH):
        pool_bd = jax.scipy.linalg.block_diag(*[pool_w[l, g] for g in range(len(POOL_WINDOWS))])
        sgu_wide = jnp.transpose(sgu_w[l], (1, 0, 2)).reshape(SGU_BLOCK, 4 * SGU_BLOCK)
        sgu_bias_tbl = jnp.repeat(sgu_bias[l].T, HEAD, axis=1)
        params = (
            w_in[l].astype(BF16), row(b_in[l]),
            conv_a_w[l], row(conv_a_b[l]), row(norm_a_g[l]), row(norm_a_b[l]),
            conv_b_w[l],
            pool_bd.astype(BF16), row(pool_scale[l]), corr,
            row(sgu_ln_g[l]), row(sgu_ln_b[l]), sgu_wide, sgu_bias_tbl,
            w_out[l].astype(BF16), row(b_out[l]), row(ln_g[l]), row(ln_b[l]),
            phead,
        )
        h = _layer(h, seq_len, params)
    return h.reshape(bsz, seq_len, d)
```

```python
import functools

import jax
import jax.numpy as jnp
import numpy as np
from jax.experimental import pallas as pl
from jax.experimental.pallas import tpu as pltpu

D_MODEL = 1024
DEPTH = 4
GROUP = 256
HEAD = 64
CONV_A_K = 31
CONV_B_K = 3
POOL_WINDOWS = (2, 4, 8, 16)
SGU_BLOCK = 128
CHUNK = 64
LN_EPS = 1e-5
ALPHA = float((2.0 * DEPTH) ** 0.25)

LANES = 128
SUBLANES = 8
TILE_ROWS = 512
MIX_ROWS = SGU_BLOCK
OUT_ROWS = 256
HALO_A = 32
HALO_B = 8
HALO_C = 16
PAD_C = 8
N_KEPT = 5
WEIGHT_CAST_ROWS = 128
CONV_A_UNROLL = 4
VMEM_LIMIT_BYTES = 56 * 1024 * 1024

F32 = jnp.float32
BF16 = jnp.bfloat16


def _silu(v):
    return v * jax.nn.sigmoid(v)


def _dot(a, b):
    return jnp.dot(a, b, preferred_element_type=F32)


def _after(value, anchor, slow=False):
    bits = jax.lax.bitcast_convert_type(anchor, jnp.uint32)
    zero = jax.lax.shift_right_logical(jax.lax.shift_right_logical(bits, jnp.uint32(16)), jnp.uint32(16))
    if slow:
        total = jnp.sum(jax.lax.bitcast_convert_type(zero, F32), axis=-1, keepdims=True)
        zero = jax.lax.bitcast_convert_type(jnp.broadcast_to(total, value.shape), jnp.uint32)
    return jax.lax.bitcast_convert_type(jax.lax.bitcast_convert_type(value, jnp.uint32) | zero, F32)


def _layer_norm_rows(v, g, b):
    mu = jnp.mean(v, axis=-1, keepdims=True)
    d = v - mu
    var = jnp.mean(d * d, axis=-1, keepdims=True)
    return d * jax.lax.rsqrt(var + LN_EPS) * g + b


def _layer_kernel(x_ref, xprev_ref, win_ref, bin_ref, caw_ref, cab_ref, nag_ref, nab_ref, cbw_ref,
                  pbd_ref, psc_ref, sg_ref, sb_ref, swb_ref, sbias_ref,
                  wout_ref, bout_ref, lng_ref, lnb_ref, corr_ref, phead_ref,
                  o_ref,
                  abuf0, bbuf0, cbuf0, kept0, abuf1, bbuf1, cbuf1, kept1,
                  s2buf, s4buf, s8buf, tbuf, mix, xb_ref, ws_ref, win_bf, wout_bf,
                  *, tiles_per_seq):
    pid = pl.program_id(0)
    slots = ((abuf0, bbuf0, cbuf0, kept0), (abuf1, bbuf1, cbuf1, kept1))

    @pl.when(pid == 0)
    def _init():
        def cast_rows(k, carry):
            rs = pl.ds(pl.multiple_of(k * WEIGHT_CAST_ROWS, WEIGHT_CAST_ROWS), WEIGHT_CAST_ROWS)
            win_bf[rs, :] = win_ref[rs, :].astype(BF16)
            wout_bf[rs, :] = wout_ref[rs, :].astype(BF16)
            return carry
        jax.lax.fori_loop(0, D_MODEL // WEIGHT_CAST_ROWS, cast_rows, 0)
        abuf1[...] = jnp.zeros(abuf1.shape, F32)
        bbuf1[...] = jnp.zeros(bbuf1.shape, F32)
        cbuf0[...] = jnp.zeros(cbuf0.shape, F32)
        cbuf1[...] = jnp.zeros(cbuf1.shape, F32)
        kept1[...] = jnp.zeros(kept1.shape, F32)
        s2buf[:, 0:PAD_C, :] = jnp.zeros((2, PAD_C, LANES), F32)
        wi = jax.lax.broadcasted_iota(jnp.int32, (SGU_BLOCK, 4 * SGU_BLOCK), 0)
        wj = jax.lax.broadcasted_iota(jnp.int32, (SGU_BLOCK, 4 * SGU_BLOCK), 1) % SGU_BLOCK
        ws_ref[...] = jnp.where(wj // CHUNK <= wi // CHUNK, swb_ref[...], 0.0).astype(BF16)

    refs = (x_ref, xprev_ref, win_bf, bin_ref, caw_ref, cab_ref, nag_ref, nab_ref, cbw_ref,
            pbd_ref, psc_ref, corr_ref, sg_ref, sb_ref, sbias_ref, wout_bf, bout_ref, lng_ref, lnb_ref,
            phead_ref, o_ref, s2buf, s4buf, s8buf, tbuf, mix, xb_ref, ws_ref)
    for parity in range(2):
        @pl.when(pid % 2 == parity)
        def _step(parity=parity):
            _pipeline_step(pid, tiles_per_seq, slots[parity], slots[1 - parity], *refs)


def _pipeline_step(pid, tiles_per_seq, cur_slot, prev_slot,
                   x_ref, xprev_ref, win_ref, bin_ref, caw_ref, cab_ref, nag_ref, nab_ref, cbw_ref,
                   pbd_ref, psc_ref, corr_ref, sg_ref, sb_ref, sbias_ref, wout_ref, bout_ref, lng_ref, lnb_ref,
                   phead_ref, o_ref, s2buf, s4buf, s8buf, tbuf, mix, xb_ref, ws_ref):
    rows = TILE_ROWS
    dz = jnp.minimum(pid, 0)
    abuf_w, bbuf_w, cbuf_w, kept_w = cur_slot
    abuf_r, bbuf_r, cbuf_r, kept_r = prev_slot
    c0 = PAD_C + HALO_C

    def shifted(ref, start, n):
        return ref[pl.ds(dz + start, n), :]

    xb_ref[...] = x_ref[...].astype(BF16)

    def proj(i):
        lo, hi = GROUP * i, GROUP * (i + 1)
        return _dot(xb_ref[...], win_ref[:, lo:hi]) + bin_ref[:, lo:hi]

    def put_slabs(buf, halo, val):
        for s in range(2):
            buf[s, halo:halo + rows, :] = val[:, LANES * s:LANES * (s + 1)]

    def proj_a():
        put_slabs(abuf_w, HALO_A, proj(0) * jax.nn.sigmoid(proj(1)))

    def proj_a_gate():
        kept_w[0] = _silu(proj(2))

    def proj_b_gate():
        kept_w[1] = proj(3) * _silu(proj(6))

    def proj_b():
        put_slabs(bbuf_w, HALO_B, proj(4) * proj(5))

    def proj_c():
        put_slabs(cbuf_w, c0, proj(7))

    def proj_c_gate():
        kept_w[2] = _silu(proj(8)) * psc_ref[...]

    def proj_d_gate():
        kept_w[3] = proj(9) * _silu(proj(11))

    def proj_d():
        kept_w[4] = proj(10)

    first_prev = ((pid - 1) % tiles_per_seq) == 0
    csrc = [cbuf_r.at[s] for s in range(2)]

    def pool_sums():
        n2 = rows + HALO_C
        for s in range(2):
            s2buf[s, PAD_C:PAD_C + n2, :] = shifted(csrc[s], PAD_C, n2) + shifted(csrc[s], PAD_C - 1, n2)
        for s in range(2):
            s4buf[s, PAD_C:PAD_C + n2, :] = (
                shifted(s2buf.at[s], PAD_C, n2) + shifted(s2buf.at[s], PAD_C - 2, n2))
        n8 = rows + 8
        s8buf[c0 - 8:c0 - 8 + n8, :] = shifted(s4buf.at[1], c0 - 8, n8) + shifted(s4buf.at[1], c0 - 12, n8)

    def half(h):
        return slice(OUT_ROWS * h, OUT_ROWS * (h + 1))

    conv_a_tail = []

    def conv_a(h):
        for s in range(2):
            lo, hi = LANES * s, LANES * (s + 1)
            src = abuf_r.at[s]
            bias = jnp.broadcast_to(cab_ref[:, lo:hi], (SUBLANES, LANES))
            for g in range(OUT_ROWS // (SUBLANES * CONV_A_UNROLL)):
                g0 = OUT_ROWS * h + SUBLANES * CONV_A_UNROLL * g
                base = HALO_A + g0 - (CONV_A_K - 1)
                start = _after(bias, conv_a_tail[-1], slow=len(conv_a_tail) % 2 == 0) if conv_a_tail else bias
                accs = [start] * CONV_A_UNROLL
                for k in range(CONV_A_K):
                    wk = jnp.broadcast_to(caw_ref[k:k + 1, lo:hi], (SUBLANES, LANES))
                    for j in range(CONV_A_UNROLL):
                        accs[j] = accs[j] + shifted(src, base + SUBLANES * j + k, SUBLANES) * wk
                for j in range(CONV_A_UNROLL):
                    tbuf[g0 + SUBLANES * j:g0 + SUBLANES * (j + 1), lo:hi] = accs[j]
                conv_a_tail.append(accs[-1])

    def prep_cd(h):
        n, r0 = OUT_ROWS, OUT_ROWS * h
        low_half = jax.lax.broadcasted_iota(jnp.int32, (1, LANES), 1) < HEAD
        cr = c0 + r0
        s16 = shifted(s8buf, cr, n) + shifted(s8buf, cr - 8, n)
        mean0 = jnp.where(low_half, s2buf[0, cr:cr + n, :] * 0.5, s4buf[0, cr:cr + n, :] * 0.25)
        mean1 = jnp.where(low_half, s8buf[cr:cr + n, :] * 0.125, s16 * 0.0625)
        mean = jnp.concatenate([mean0, mean1], axis=-1)
        if h == 0:
            head_rows = mean[0:HALO_C, :] * jnp.where(first_prev, corr_ref[...], 1.0)
            mean = jnp.concatenate([head_rows, mean[HALO_C:, :]], axis=0)
        ch = jnp.concatenate([shifted(csrc[s], cr, n) for s in range(2)], axis=-1)
        pooled = (mean - ch).astype(BF16)
        v = _layer_norm_rows(kept_r[4, half(h), :], sg_ref[...], sb_ref[...]).astype(BF16)
        return pooled, v

    def mean_a(h):
        return _dot(tbuf[half(h), :].astype(BF16), phead_ref[...])

    def var_a(h, mu):
        d = tbuf[half(h), :] - mu
        tbuf[half(h), :] = d
        return _dot((d * d).astype(BF16), phead_ref[...])

    def dots_cd(pooled, v):
        yc = _dot(pooled, pbd_ref[...])
        head_of_lane = jax.lax.broadcasted_iota(jnp.int32, (1, GROUP), 1) // HEAD
        sp = []
        for q in range(OUT_ROWS // SGU_BLOCK):
            blk = v[SGU_BLOCK * q:SGU_BLOCK * (q + 1), :]
            vbig = jnp.concatenate(
                [jnp.where(head_of_lane == hd, blk, jnp.zeros_like(blk)) for hd in range(4)], axis=0)
            sp.append(_dot(ws_ref[...], vbig) + sbias_ref[...])
        return yc, jnp.concatenate(sp, axis=0)

    def finish(h, var, yc, sp):
        rs = half(h)
        r0, n = OUT_ROWS * h, OUT_ROWS
        an = tbuf[rs, :] * jax.lax.rsqrt(var + LN_EPS) * nag_ref[...] + nab_ref[...]
        mix[rs, 0:GROUP] = (_silu(an) * kept_r[0, rs, :]).astype(BF16)
        conv_b = []
        for s in range(2):
            lo, hi = LANES * s, LANES * (s + 1)
            src = bbuf_r.at[s]
            acc = shifted(src, HALO_B + r0, n) * cbw_ref[CONV_B_K - 1:CONV_B_K, lo:hi]
            for k in range(CONV_B_K - 1):
                acc = acc + shifted(src, HALO_B + r0 - (CONV_B_K - 1) + k, n) * cbw_ref[k:k + 1, lo:hi]
            conv_b.append(acc)
        mix[rs, GROUP:2 * GROUP] = (jnp.concatenate(conv_b, axis=-1) * kept_r[1, rs, :]).astype(BF16)
        mix[rs, 2 * GROUP:3 * GROUP] = (yc * kept_r[2, rs, :]).astype(BF16)
        mix[rs, 3 * GROUP:4 * GROUP] = (sp * kept_r[3, rs, :]).astype(BF16)

    def out_proj(h):
        o_ref[half(h), :] = _dot(mix[half(h), :], wout_ref[...]) + bout_ref[...]

    def post_norm(h):
        for q in range(OUT_ROWS * h, OUT_ROWS * (h + 1), MIX_ROWS):
            rs = slice(q, q + MIX_ROWS)
            r = ALPHA * xprev_ref[rs, :] + o_ref[rs, :]
            o_ref[rs, :] = _layer_norm_rows(r, lng_ref[...], lnb_ref[...])

    pool_sums()
    conv_a(0)
    conv_a(1)
    proj_a()
    proj_a_gate()
    proj_b_gate()
    proj_b()
    proj_c()
    pooled0, v0 = prep_cd(0)
    pooled1, v1 = prep_cd(1)
    mu0 = mean_a(0)
    mu1 = mean_a(1)
    proj_c_gate()
    var0 = var_a(0, mu0)
    var1 = var_a(1, mu1)
    yc0, sp0 = dots_cd(pooled0, v0)
    yc1, sp1 = dots_cd(pooled1, v1)
    proj_d()
    finish(0, var0, yc0, sp0)
    out_proj(0)
    finish(1, var1, yc1, sp1)
    out_proj(1)
    post_norm(0)
    proj_d_gate()
    post_norm(1)

    fresh = (pid % tiles_per_seq) == 0
    abuf_w[:, 0:HALO_A, :] = jnp.where(fresh, 0.0, abuf_r[:, rows:rows + HALO_A, :])
    bbuf_w[:, 0:HALO_B, :] = jnp.where(fresh, 0.0, bbuf_r[:, rows:rows + HALO_B, :])
    cbuf_w[:, PAD_C:c0, :] = jnp.where(fresh, 0.0, cbuf_r[:, rows + PAD_C:rows + c0, :])


def _layer_param_spec(layer, shape):
    return pl.BlockSpec((None,) + tuple(shape[1:]), lambda i: (layer,) + (0,) * (len(shape) - 1))


def _shared_spec(shape):
    return pl.BlockSpec(shape, lambda i: (0,) * len(shape))


def _layer(x2, seq_len, layer, stacked, shared):
    n = x2.shape[0]
    assert seq_len % TILE_ROWS == 0 and TILE_ROWS == 2 * OUT_ROWS and OUT_ROWS % MIX_ROWS == 0
    n_tiles = n // TILE_ROWS
    tile = (TILE_ROWS, D_MODEL)
    in_specs = [pl.BlockSpec(tile, lambda i: (jnp.minimum(i, n_tiles - 1), 0)),
                pl.BlockSpec(tile, lambda i: (jnp.maximum(i - 1, 0), 0))]
    in_specs += [_layer_param_spec(layer, p.shape) for p in stacked]
    in_specs += [_shared_spec(p.shape) for p in shared]
    hist = lambda halo: pltpu.VMEM((2, halo + TILE_ROWS, LANES), F32)
    slot = [
        hist(HALO_A),
        hist(HALO_B),
        hist(PAD_C + HALO_C),
        pltpu.VMEM((N_KEPT, TILE_ROWS, GROUP), F32),
    ]
    work = lambda *lead: pltpu.VMEM((*lead, PAD_C + HALO_C + TILE_ROWS, LANES), F32)
    return pl.pallas_call(
        functools.partial(_layer_kernel, tiles_per_seq=seq_len // TILE_ROWS),
        grid=(n_tiles + 1,),
        in_specs=in_specs,
        out_specs=pl.BlockSpec(tile, lambda i: (jnp.maximum(i - 1, 0), 0)),
        out_shape=jax.ShapeDtypeStruct((n, D_MODEL), F32),
        scratch_shapes=slot + slot + [
            work(2),
            work(2),
            work(),
            pltpu.VMEM((TILE_ROWS, GROUP), F32),
            pltpu.VMEM((TILE_ROWS, D_MODEL), BF16),
            pltpu.VMEM((TILE_ROWS, D_MODEL), BF16),
            pltpu.VMEM((SGU_BLOCK, 4 * SGU_BLOCK), BF16),
            pltpu.VMEM((D_MODEL, 12 * GROUP), BF16),
            pltpu.VMEM((D_MODEL, D_MODEL), BF16),
        ],
        compiler_params=pltpu.CompilerParams(
            dimension_semantics=("arbitrary",), vmem_limit_bytes=VMEM_LIMIT_BYTES),
        name="hybrid_layer",
    )(x2, x2, *stacked, *shared)


def _constants():
    head_id = np.arange(GROUP) // HEAD
    phead = (head_id[:, None] == head_id[None, :]).astype(np.float32) / HEAD
    win = np.asarray(POOL_WINDOWS, np.float32)[head_id]
    pos1 = np.arange(1, HALO_C + 1, dtype=np.float32)[:, None]
    corr = win[None, :] / np.minimum(pos1, win[None, :])
    return jnp.asarray(phead, BF16), jnp.asarray(corr, F32)


def kernel(x, ln_g, ln_b, w_in, b_in, conv_a_w, conv_a_b, norm_a_g, norm_a_b, conv_b_w, pool_w, pool_scale, sgu_ln_g, sgu_ln_b, sgu_w, sgu_bias, w_out, b_out):
    bsz, seq_len, d = x.shape
    depth = w_in.shape[0]
    phead, corr = _constants()
    rows = lambda v: v.reshape(depth, 1, -1)
    pool_bd = jnp.zeros((depth, GROUP, GROUP), F32)
    for g in range(len(POOL_WINDOWS)):
        pool_bd = pool_bd.at[:, HEAD * g:HEAD * (g + 1), HEAD * g:HEAD * (g + 1)].set(pool_w[:, g])
    sgu_wide = jnp.transpose(sgu_w, (0, 2, 1, 3)).reshape(depth, SGU_BLOCK, 4 * SGU_BLOCK)
    sgu_bias_tbl = jnp.repeat(jnp.transpose(sgu_bias, (0, 2, 1)), HEAD, axis=2)
    stacked = (
        w_in, rows(b_in),
        conv_a_w, rows(conv_a_b), rows(norm_a_g), rows(norm_a_b),
        conv_b_w,
        pool_bd.astype(BF16), rows(pool_scale),
        rows(sgu_ln_g), rows(sgu_ln_b), sgu_wide, sgu_bias_tbl,
        w_out, rows(b_out), rows(ln_g), rows(ln_b),
    )
    shared = (corr, phead)
    h = x.reshape(bsz * seq_len, d)
    for layer in range(depth):
        h = _layer(h, seq_len, layer, stacked, shared)
    return h.reshape(bsz, seq_len, d)
```

```python
import functools

import jax
import jax.numpy as jnp
import numpy as np
from jax.experimental import pallas as pl
from jax.experimental.pallas import tpu as pltpu

D_MODEL = 1024
DEPTH = 4
GROUP = 256
HEAD = 64
CONV_A_K = 31
CONV_B_K = 3
POOL_WINDOWS = (2, 4, 8, 16)
SGU_BLOCK = 128
CHUNK = 64
LN_EPS = 1e-5
ALPHA = float((2.0 * DEPTH) ** 0.25)

LANES = 128
SUBLANES = 8
TILE_ROWS = 512
NORM_ROWS = SGU_BLOCK
OUT_ROWS = 256
HALO_A = 32
HALO_B = 8
HALO_C = 16
PAD_C = 8
N_KEPT = 5
WEIGHT_CAST_ROWS = 128
CONV_A_UNROLL = 4
VMEM_LIMIT_BYTES = 56 * 1024 * 1024
N_LAYER_PARAMS = 17
N_SLOT_BUFFERS = 4

F32 = jnp.float32
BF16 = jnp.bfloat16


def _silu(v):
    return v * jax.nn.sigmoid(v)


def _dot(a, b):
    return jnp.dot(a, b, preferred_element_type=F32)


def _after(value, anchor, slow=False):
    bits = jax.lax.bitcast_convert_type(anchor, jnp.uint32)
    zero = jax.lax.shift_right_logical(jax.lax.shift_right_logical(bits, jnp.uint32(16)), jnp.uint32(16))
    if slow:
        total = jnp.sum(jax.lax.bitcast_convert_type(zero, F32), axis=-1, keepdims=True)
        zero = jax.lax.bitcast_convert_type(jnp.broadcast_to(total, value.shape), jnp.uint32)
    return jax.lax.bitcast_convert_type(jax.lax.bitcast_convert_type(value, jnp.uint32) | zero, F32)


def _layer_norm_rows(v, g, b):
    mu = jnp.mean(v, axis=-1, keepdims=True)
    d = v - mu
    var = jnp.mean(d * d, axis=-1, keepdims=True)
    return d * jax.lax.rsqrt(var + LN_EPS) * g + b


def _layer_kernel(*refs, tiles_per_seq, n_tiles, first_layer, last_layer):
    x_ref, xprev_ref = refs[0:2]
    params = refs[2:2 + N_LAYER_PARAMS]
    corr_ref, phead_ref = refs[2 + N_LAYER_PARAMS:4 + N_LAYER_PARAMS]
    rest = refs[4 + N_LAYER_PARAMS:]
    o_ref = rest[0]
    ob_ref = None if last_layer else rest[1]
    scratch = rest[(1 if last_layer else 2):]
    slots = (scratch[0:N_SLOT_BUFFERS], scratch[N_SLOT_BUFFERS:2 * N_SLOT_BUFFERS])
    s2buf, s4buf, s8buf, tbuf, mix, ws_ref, win_bf, wout_bf = scratch[2 * N_SLOT_BUFFERS:2 * N_SLOT_BUFFERS + 8]
    xb_ref = scratch[2 * N_SLOT_BUFFERS + 8] if first_layer else None
    win_ref, swb_ref, wout_ref = params[0], params[11], params[13]
    pid = pl.program_id(0)

    step = functools.partial(
        _pipeline_step, pid, tiles_per_seq, x_ref, xprev_ref, params, corr_ref, phead_ref, o_ref, ob_ref,
        s2buf, s4buf, s8buf, tbuf, mix, ws_ref, win_bf, wout_bf, xb_ref)

    @pl.when(pid == 0)
    def _first():
        def cast_rows(k, carry):
            rs = pl.ds(pl.multiple_of(k * WEIGHT_CAST_ROWS, WEIGHT_CAST_ROWS), WEIGHT_CAST_ROWS)
            win_bf[rs, :] = win_ref[rs, :].astype(BF16)
            wout_bf[rs, :] = wout_ref[rs, :].astype(BF16)
            return carry
        jax.lax.fori_loop(0, D_MODEL // WEIGHT_CAST_ROWS, cast_rows, 0)
        for slot in slots:
            slot[2][:, 0:PAD_C, :] = jnp.zeros((2, PAD_C, LANES), F32)
        s2buf[:, 0:PAD_C, :] = jnp.zeros((2, PAD_C, LANES), F32)
        wi = jax.lax.broadcasted_iota(jnp.int32, (SGU_BLOCK, 4 * SGU_BLOCK), 0)
        wj = jax.lax.broadcasted_iota(jnp.int32, (SGU_BLOCK, 4 * SGU_BLOCK), 1) % SGU_BLOCK
        ws_ref[...] = jnp.where(wj // CHUNK <= wi // CHUNK, swb_ref[...], 0.0).astype(BF16)
        step(slots[0], None)

    for parity in range(2):
        @pl.when((pid > 0) & (pid < n_tiles) & (pid % 2 == parity))
        def _middle(parity=parity):
            step(slots[parity], slots[1 - parity])

    @pl.when(pid == n_tiles)
    def _last():
        step(None, slots[(n_tiles - 1) % 2])


def _pipeline_step(pid, tiles_per_seq, x_ref, xprev_ref, params, corr_ref, phead_ref, o_ref, ob_ref,
                   s2buf, s4buf, s8buf, tbuf, mix, ws_ref, win_ref, wout_ref, xb_ref,
                   cur_slot, prev_slot):
    (_, bin_ref, caw_ref, cab_ref, nag_ref, nab_ref, cbw_ref, pbd_ref, psc_ref,
     sg_ref, sb_ref, _, sbias_ref, _, bout_ref, lng_ref, lnb_ref) = params
    rows = TILE_ROWS
    dz = jnp.minimum(pid, 0)
    c0 = PAD_C + HALO_C

    def shifted(ref, start, n):
        return ref[pl.ds(dz + start, n), :]

    if cur_slot is not None:
        abuf_w, bbuf_w, cbuf_w, kept_w = cur_slot
        if xb_ref is None:
            lhs_ref = x_ref
        else:
            xb_ref[...] = x_ref[...].astype(BF16)
            lhs_ref = xb_ref

    def proj(i):
        lo, hi = GROUP * i, GROUP * (i + 1)
        return _dot(lhs_ref[...], win_ref[:, lo:hi]) + bin_ref[:, lo:hi]

    def put_slabs(buf, halo, val):
        for s in range(2):
            buf[s, halo:halo + rows, :] = val[:, LANES * s:LANES * (s + 1)]

    def proj_a():
        put_slabs(abuf_w, HALO_A, proj(0) * jax.nn.sigmoid(proj(1)))

    def proj_a_gate():
        kept_w[0] = _silu(proj(2))

    def proj_b_gate():
        kept_w[1] = proj(3) * _silu(proj(6))

    def proj_b():
        put_slabs(bbuf_w, HALO_B, proj(4) * proj(5))

    def proj_c():
        put_slabs(cbuf_w, c0, proj(7))

    def proj_c_gate():
        kept_w[2] = _silu(proj(8)) * psc_ref[...]

    def proj_d_gate():
        kept_w[3] = proj(9) * _silu(proj(11))

    def proj_d():
        kept_w[4] = proj(10)

    if prev_slot is not None:
        abuf_r, bbuf_r, cbuf_r, kept_r = prev_slot
        first_prev = ((pid - 1) % tiles_per_seq) == 0
        csrc = [cbuf_r.at[s] for s in range(2)]

    def pool_sums():
        n2 = rows + HALO_C
        for s in range(2):
            s2buf[s, PAD_C:PAD_C + n2, :] = shifted(csrc[s], PAD_C, n2) + shifted(csrc[s], PAD_C - 1, n2)
        for s in range(2):
            s4buf[s, PAD_C:PAD_C + n2, :] = (
                shifted(s2buf.at[s], PAD_C, n2) + shifted(s2buf.at[s], PAD_C - 2, n2))
        n8 = rows + 8
        s8buf[c0 - 8:c0 - 8 + n8, :] = shifted(s4buf.at[1], c0 - 8, n8) + shifted(s4buf.at[1], c0 - 12, n8)

    def half(h):
        return slice(OUT_ROWS * h, OUT_ROWS * (h + 1))

    conv_a_tail = []

    def conv_a(h):
        for s in range(2):
            lo, hi = LANES * s, LANES * (s + 1)
            src = abuf_r.at[s]
            bias = jnp.broadcast_to(cab_ref[:, lo:hi], (SUBLANES, LANES))
            for g in range(OUT_ROWS // (SUBLANES * CONV_A_UNROLL)):
                g0 = OUT_ROWS * h + SUBLANES * CONV_A_UNROLL * g
                base = HALO_A + g0 - (CONV_A_K - 1)
                start = _after(bias, conv_a_tail[-1], slow=len(conv_a_tail) % 2 == 0) if conv_a_tail else bias
                accs = [start] * CONV_A_UNROLL
                for k in range(CONV_A_K):
                    wk = jnp.broadcast_to(caw_ref[k:k + 1, lo:hi], (SUBLANES, LANES))
                    for j in range(CONV_A_UNROLL):
                        accs[j] = accs[j] + shifted(src, base + SUBLANES * j + k, SUBLANES) * wk
                for j in range(CONV_A_UNROLL):
                    tbuf[g0 + SUBLANES * j:g0 + SUBLANES * (j + 1), lo:hi] = accs[j]
                conv_a_tail.append(accs[-1])

    def prep_cd(h):
        n, r0 = OUT_ROWS, OUT_ROWS * h
        low_half = jax.lax.broadcasted_iota(jnp.int32, (1, LANES), 1) < HEAD
        cr = c0 + r0
        s16 = shifted(s8buf, cr, n) + shifted(s8buf, cr - 8, n)
        mean0 = jnp.where(low_half, s2buf[0, cr:cr + n, :] * 0.5, s4buf[0, cr:cr + n, :] * 0.25)
        mean1 = jnp.where(low_half, s8buf[cr:cr + n, :] * 0.125, s16 * 0.0625)
        mean = jnp.concatenate([mean0, mean1], axis=-1)
        if h == 0:
            head_rows = mean[0:HALO_C, :] * jnp.where(first_prev, corr_ref[...], 1.0)
            mean = jnp.concatenate([head_rows, mean[HALO_C:, :]], axis=0)
        ch = jnp.concatenate([shifted(csrc[s], cr, n) for s in range(2)], axis=-1)
        pooled = (mean - ch).astype(BF16)
        v = _layer_norm_rows(kept_r[4, half(h), :], sg_ref[...], sb_ref[...]).astype(BF16)
        return pooled, v

    def mean_a(h):
        return _dot(tbuf[half(h), :].astype(BF16), phead_ref[...])

    def var_a(h, mu):
        d = tbuf[half(h), :] - mu
        tbuf[half(h), :] = d
        return _dot((d * d).astype(BF16), phead_ref[...])

    def dots_cd(pooled, v):
        yc = _dot(pooled, pbd_ref[...])
        head_of_lane = jax.lax.broadcasted_iota(jnp.int32, (1, GROUP), 1) // HEAD
        sp = []
        for q in range(OUT_ROWS // SGU_BLOCK):
            blk = v[SGU_BLOCK * q:SGU_BLOCK * (q + 1), :]
            vbig = jnp.concatenate(
                [jnp.where(head_of_lane == hd, blk, jnp.zeros_like(blk)) for hd in range(4)], axis=0)
            sp.append(_dot(ws_ref[...], vbig) + sbias_ref[...])
        return yc, jnp.concatenate(sp, axis=0)

    def finish(h, var, yc, sp):
        rs = half(h)
        r0, n = OUT_ROWS * h, OUT_ROWS
        an = tbuf[rs, :] * jax.lax.rsqrt(var + LN_EPS) * nag_ref[...] + nab_ref[...]
        mix[rs, 0:GROUP] = (_silu(an) * kept_r[0, rs, :]).astype(BF16)
        conv_b = []
        for s in range(2):
            lo, hi = LANES * s, LANES * (s + 1)
            src = bbuf_r.at[s]
            acc = shifted(src, HALO_B + r0, n) * cbw_ref[CONV_B_K - 1:CONV_B_K, lo:hi]
            for k in range(CONV_B_K - 1):
                acc = acc + shifted(src, HALO_B + r0 - (CONV_B_K - 1) + k, n) * cbw_ref[k:k + 1, lo:hi]
            conv_b.append(acc)
        mix[rs, GROUP:2 * GROUP] = (jnp.concatenate(conv_b, axis=-1) * kept_r[1, rs, :]).astype(BF16)
        mix[rs, 2 * GROUP:3 * GROUP] = (yc * kept_r[2, rs, :]).astype(BF16)
        mix[rs, 3 * GROUP:4 * GROUP] = (sp * kept_r[3, rs, :]).astype(BF16)

    def out_proj(h):
        o_ref[half(h), :] = _dot(mix[half(h), :], wout_ref[...]) + bout_ref[...]

    def post_norm(h):
        for q in range(OUT_ROWS * h, OUT_ROWS * (h + 1), NORM_ROWS):
            rs = slice(q, q + NORM_ROWS)
            r = ALPHA * xprev_ref[rs, :] + o_ref[rs, :]
            y = _layer_norm_rows(r, lng_ref[...], lnb_ref[...])
            o_ref[rs, :] = y
            if ob_ref is not None:
                ob_ref[rs, :] = y.astype(BF16)

    do1, do2 = cur_slot is not None, prev_slot is not None
    state = {}
    schedule = (
        (2, pool_sums), (2, lambda: conv_a(0)), (2, lambda: conv_a(1)),
        (1, proj_a), (1, proj_a_gate), (1, proj_b_gate), (1, proj_b), (1, proj_c),
        (2, lambda: state.update(cd0=prep_cd(0), cd1=prep_cd(1))),
        (2, lambda: state.update(mu0=mean_a(0), mu1=mean_a(1))),
        (1, proj_c_gate),
        (2, lambda: state.update(var0=var_a(0, state["mu0"]), var1=var_a(1, state["mu1"]))),
        (2, lambda: state.update(y0=dots_cd(*state["cd0"]), y1=dots_cd(*state["cd1"]))),
        (1, proj_d),
        (2, lambda: finish(0, state["var0"], *state["y0"])), (2, lambda: out_proj(0)),
        (2, lambda: finish(1, state["var1"], *state["y1"])), (2, lambda: out_proj(1)),
        (2, lambda: post_norm(0)),
        (1, proj_d_gate),
        (2, lambda: post_norm(1)),
    )
    for stage, piece in schedule:
        if (stage == 1 and do1) or (stage == 2 and do2):
            piece()

    if do1:
        if do2:
            fresh = (pid % tiles_per_seq) == 0
            abuf_w[:, 0:HALO_A, :] = jnp.where(fresh, 0.0, abuf_r[:, rows:rows + HALO_A, :])
            bbuf_w[:, 0:HALO_B, :] = jnp.where(fresh, 0.0, bbuf_r[:, rows:rows + HALO_B, :])
            cbuf_w[:, PAD_C:c0, :] = jnp.where(fresh, 0.0, cbuf_r[:, rows + PAD_C:rows + c0, :])
        else:
            abuf_w[:, 0:HALO_A, :] = jnp.zeros((2, HALO_A, LANES), F32)
            bbuf_w[:, 0:HALO_B, :] = jnp.zeros((2, HALO_B, LANES), F32)
            cbuf_w[:, PAD_C:c0, :] = jnp.zeros((2, HALO_C, LANES), F32)


def _layer_param_spec(layer, shape):
    return pl.BlockSpec((None,) + tuple(shape[1:]), lambda i: (layer,) + (0,) * (len(shape) - 1))


def _shared_spec(shape):
    return pl.BlockSpec(shape, lambda i: (0,) * len(shape))


def _layer(x_lhs, x_res, seq_len, layer, n_layers, stacked, shared):
    n = x_res.shape[0]
    assert seq_len % TILE_ROWS == 0 and TILE_ROWS == 2 * OUT_ROWS and OUT_ROWS % SGU_BLOCK == 0
    assert len(stacked) == N_LAYER_PARAMS
    first_layer, last_layer = layer == 0, layer == n_layers - 1
    n_tiles = n // TILE_ROWS
    tile = (TILE_ROWS, D_MODEL)
    in_specs = [pl.BlockSpec(tile, lambda i: (jnp.minimum(i, n_tiles - 1), 0)),
                pl.BlockSpec(tile, lambda i: (jnp.maximum(i - 1, 0), 0))]
    in_specs += [_layer_param_spec(layer, p.shape) for p in stacked]
    in_specs += [_shared_spec(p.shape) for p in shared]
    out_spec = pl.BlockSpec(tile, lambda i: (jnp.maximum(i - 1, 0), 0))
    out_specs, out_shape = [out_spec], [jax.ShapeDtypeStruct((n, D_MODEL), F32)]
    if not last_layer:
        out_specs.append(out_spec)
        out_shape.append(jax.ShapeDtypeStruct((n, D_MODEL), BF16))
    hist = lambda halo: pltpu.VMEM((2, halo + TILE_ROWS, LANES), F32)
    slot = [
        hist(HALO_A),
        hist(HALO_B),
        hist(PAD_C + HALO_C),
        pltpu.VMEM((N_KEPT, TILE_ROWS, GROUP), F32),
    ]
    assert len(slot) == N_SLOT_BUFFERS
    work = lambda *lead: pltpu.VMEM((*lead, PAD_C + HALO_C + TILE_ROWS, LANES), F32)
    scratch = slot + slot + [
        work(2),
        work(2),
        work(),
        pltpu.VMEM((TILE_ROWS, GROUP), F32),
        pltpu.VMEM((TILE_ROWS, D_MODEL), BF16),
        pltpu.VMEM((SGU_BLOCK, 4 * SGU_BLOCK), BF16),
        pltpu.VMEM((D_MODEL, 12 * GROUP), BF16),
        pltpu.VMEM((D_MODEL, D_MODEL), BF16),
    ]
    if first_layer:
        scratch.append(pltpu.VMEM((TILE_ROWS, D_MODEL), BF16))
    outs = pl.pallas_call(
        functools.partial(_layer_kernel, tiles_per_seq=seq_len // TILE_ROWS, n_tiles=n_tiles,
                          first_layer=first_layer, last_layer=last_layer),
        grid=(n_tiles + 1,),
        in_specs=in_specs,
        out_specs=out_specs,
        out_shape=out_shape,
        scratch_shapes=scratch,
        compiler_params=pltpu.CompilerParams(
            dimension_semantics=("arbitrary",), vmem_limit_bytes=VMEM_LIMIT_BYTES),
        name="hybrid_layer",
    )(x_lhs, x_res, *stacked, *shared)
    return outs[0], (None if last_layer else outs[1])


def _constants():
    head_id = np.arange(GROUP) // HEAD
    phead = (head_id[:, None] == head_id[None, :]).astype(np.float32) / HEAD
    win = np.asarray(POOL_WINDOWS, np.float32)[head_id]
    pos1 = np.arange(1, HALO_C + 1, dtype=np.float32)[:, None]
    corr = win[None, :] / np.minimum(pos1, win[None, :])
    return jnp.asarray(phead, BF16), jnp.asarray(corr, F32)


def kernel(x, ln_g, ln_b, w_in, b_in, conv_a_w, conv_a_b, norm_a_g, norm_a_b, conv_b_w, pool_w, pool_scale, sgu_ln_g, sgu_ln_b, sgu_w, sgu_bias, w_out, b_out):
    bsz, seq_len, d = x.shape
    depth = w_in.shape[0]
    phead, corr = _constants()
    rows = lambda v: v.reshape(depth, 1, -1)
    pool_bd = jnp.zeros((depth, GROUP, GROUP), F32)
    for g in range(len(POOL_WINDOWS)):
        pool_bd = pool_bd.at[:, HEAD * g:HEAD * (g + 1), HEAD * g:HEAD * (g + 1)].set(pool_w[:, g])
    sgu_wide = jnp.transpose(sgu_w, (0, 2, 1, 3)).reshape(depth, SGU_BLOCK, 4 * SGU_BLOCK)
    sgu_bias_tbl = jnp.repeat(jnp.transpose(sgu_bias, (0, 2, 1)), HEAD, axis=2)
    stacked = (
        w_in, rows(b_in),
        conv_a_w, rows(conv_a_b), rows(norm_a_g), rows(norm_a_b),
        conv_b_w,
        pool_bd.astype(BF16), rows(pool_scale),
        rows(sgu_ln_g), rows(sgu_ln_b), sgu_wide, sgu_bias_tbl,
        w_out, rows(b_out), rows(ln_g), rows(ln_b),
    )
    shared = (corr, phead)
    h = x.reshape(bsz * seq_len, d)
    h_bf16 = None
    for layer in range(depth):
        h, h_bf16 = _layer(h if h_bf16 is None else h_bf16, h, seq_len, layer, depth, stacked, shared)
    return h.reshape(bsz, seq_len, d)
```

```python
import functools

import jax
import jax.numpy as jnp
import numpy as np
from jax.experimental import pallas as pl
from jax.experimental.pallas import tpu as pltpu

D_MODEL = 1024
DEPTH = 4
GROUP = 256
HEAD = 64
CONV_A_K = 31
CONV_B_K = 3
POOL_WINDOWS = (2, 4, 8, 16)
SGU_BLOCK = 128
CHUNK = 64
LN_EPS = 1e-5
ALPHA = float((2.0 * DEPTH) ** 0.25)

LANES = 128
SUBLANES = 8
TILE_ROWS = 512
NORM_ROWS = SGU_BLOCK
OUT_ROWS = 256
HALO_A = 32
HALO_B = 8
HALO_C = 16
PAD_C = 8
N_KEPT = 5
WEIGHT_CAST_ROWS = 128
CONV_A_UNROLL = 4
VMEM_LIMIT_BYTES = 56 * 1024 * 1024
N_LAYER_PARAMS = 17
N_SLOT_BUFFERS = 4

F32 = jnp.float32
BF16 = jnp.bfloat16


def _silu(v):
    return v * jax.nn.sigmoid(v)


def _dot(a, b):
    return jnp.dot(a, b, preferred_element_type=F32)


def _after(value, anchor, slow=False):
    bits = jax.lax.bitcast_convert_type(anchor, jnp.uint32)
    zero = jax.lax.shift_right_logical(jax.lax.shift_right_logical(bits, jnp.uint32(16)), jnp.uint32(16))
    if slow:
        total = jnp.sum(jax.lax.bitcast_convert_type(zero, F32), axis=-1, keepdims=True)
        zero = jax.lax.bitcast_convert_type(jnp.broadcast_to(total, value.shape), jnp.uint32)
    return jax.lax.bitcast_convert_type(jax.lax.bitcast_convert_type(value, jnp.uint32) | zero, F32)


def _layer_norm_rows(v, g, b):
    mu = jnp.mean(v, axis=-1, keepdims=True)
    d = v - mu
    var = jnp.mean(d * d, axis=-1, keepdims=True)
    return d * jax.lax.rsqrt(var + LN_EPS) * g + b


def _layer_kernel(*refs, tiles_per_seq, n_tiles, first_layer, last_layer):
    x_ref, xprev_ref = refs[0:2]
    params = refs[2:2 + N_LAYER_PARAMS]
    corr_ref, phead_ref = refs[2 + N_LAYER_PARAMS:4 + N_LAYER_PARAMS]
    rest = refs[4 + N_LAYER_PARAMS:]
    o_ref = rest[0]
    ob_ref = None if last_layer else rest[1]
    scratch = rest[(1 if last_layer else 2):]
    slots = (scratch[0:N_SLOT_BUFFERS], scratch[N_SLOT_BUFFERS:2 * N_SLOT_BUFFERS])
    s2buf, s4buf, s8buf, tbuf, mix, ws_ref, win_bf, wout_bf = scratch[2 * N_SLOT_BUFFERS:2 * N_SLOT_BUFFERS + 8]
    xb_ref = scratch[2 * N_SLOT_BUFFERS + 8] if first_layer else None
    win_ref, swb_ref, wout_ref = params[0], params[11], params[13]
    pid = pl.program_id(0)

    step = functools.partial(
        _pipeline_step, pid, tiles_per_seq, x_ref, xprev_ref, params, corr_ref, phead_ref, o_ref, ob_ref,
        s2buf, s4buf, s8buf, tbuf, mix, ws_ref, win_bf, wout_bf, xb_ref)

    @pl.when(pid == 0)
    def _first():
        def cast_rows(k, carry):
            rs = pl.ds(pl.multiple_of(k * WEIGHT_CAST_ROWS, WEIGHT_CAST_ROWS), WEIGHT_CAST_ROWS)
            win_bf[rs, :] = win_ref[rs, :].astype(BF16)
            wout_bf[rs, :] = wout_ref[rs, :].astype(BF16)
            return carry
        jax.lax.fori_loop(0, D_MODEL // WEIGHT_CAST_ROWS, cast_rows, 0)
        for slot in slots:
            slot[2][:, 0:PAD_C, :] = jnp.zeros((2, PAD_C, LANES), F32)
        s2buf[:, 0:PAD_C, :] = jnp.zeros((2, PAD_C, LANES), F32)
        wi = jax.lax.broadcasted_iota(jnp.int32, (SGU_BLOCK, 4 * SGU_BLOCK), 0)
        wj = jax.lax.broadcasted_iota(jnp.int32, (SGU_BLOCK, 4 * SGU_BLOCK), 1) % SGU_BLOCK
        ws_ref[...] = jnp.where(wj // CHUNK <= wi // CHUNK, swb_ref[...], 0.0).astype(BF16)
        step(slots[0], None)

    for parity in range(2):
        @pl.when((pid > 0) & (pid < n_tiles) & (pid % 2 == parity))
        def _middle(parity=parity):
            step(slots[parity], slots[1 - parity])

    @pl.when(pid == n_tiles)
    def _last():
        step(None, slots[(n_tiles - 1) % 2])


def _pipeline_step(pid, tiles_per_seq, x_ref, xprev_ref, params, corr_ref, phead_ref, o_ref, ob_ref,
                   s2buf, s4buf, s8buf, tbuf, mix, ws_ref, win_ref, wout_ref, xb_ref,
                   cur_slot, prev_slot):
    (_, bin_ref, caw_ref, cab_ref, nag_ref, nab_ref, cbw_ref, pbd_ref, psc_ref,
     sg_ref, sb_ref, _, sbias_ref, _, bout_ref, lng_ref, lnb_ref) = params
    rows = TILE_ROWS
    dz = jnp.minimum(pid, 0)
    c0 = PAD_C + HALO_C

    def shifted(ref, start, n):
        return ref[pl.ds(dz + start, n), :]

    if cur_slot is not None:
        abuf_w, bbuf_w, cbuf_w, kept_w = cur_slot
        if xb_ref is None:
            lhs_ref = x_ref
        else:
            xb_ref[...] = x_ref[...].astype(BF16)
            lhs_ref = xb_ref

    def proj(i):
        lo, hi = GROUP * i, GROUP * (i + 1)
        return _dot(lhs_ref[...], win_ref[:, lo:hi]) + bin_ref[:, lo:hi]

    def put_slabs(buf, halo, val):
        for s in range(2):
            buf[s, halo:halo + rows, :] = val[:, LANES * s:LANES * (s + 1)]

    def proj_a():
        put_slabs(abuf_w, HALO_A, proj(0) * jax.nn.sigmoid(proj(1)))

    def proj_a_gate():
        kept_w[0] = _silu(proj(2))

    def proj_b_gate():
        kept_w[1] = proj(3) * _silu(proj(6))

    def proj_b():
        put_slabs(bbuf_w, HALO_B, proj(4) * proj(5))

    def proj_c():
        put_slabs(cbuf_w, c0, proj(7))

    def proj_c_gate():
        kept_w[2] = _silu(proj(8)) * psc_ref[...]

    def proj_d_gate():
        kept_w[3] = proj(9) * _silu(proj(11))

    def proj_d():
        kept_w[4] = proj(10)

    if prev_slot is not None:
        abuf_r, bbuf_r, cbuf_r, kept_r = prev_slot
        first_prev = ((pid - 1) % tiles_per_seq) == 0
        csrc = [cbuf_r.at[s] for s in range(2)]

    def pool_sums():
        n2 = rows + HALO_C
        for s in range(2):
            s2buf[s, PAD_C:PAD_C + n2, :] = shifted(csrc[s], PAD_C, n2) + shifted(csrc[s], PAD_C - 1, n2)
        for s in range(2):
            s4buf[s, PAD_C:PAD_C + n2, :] = (
                shifted(s2buf.at[s], PAD_C, n2) + shifted(s2buf.at[s], PAD_C - 2, n2))
        n8 = rows + 8
        s8buf[c0 - 8:c0 - 8 + n8, :] = shifted(s4buf.at[1], c0 - 8, n8) + shifted(s4buf.at[1], c0 - 12, n8)

    def half(h):
        return slice(OUT_ROWS * h, OUT_ROWS * (h + 1))

    conv_a_tail = []

    def conv_a(h):
        for s in range(2):
            lo, hi = LANES * s, LANES * (s + 1)
            src = abuf_r.at[s]
            bias = jnp.broadcast_to(cab_ref[:, lo:hi], (SUBLANES, LANES))
            for g in range(OUT_ROWS // (SUBLANES * CONV_A_UNROLL)):
                g0 = OUT_ROWS * h + SUBLANES * CONV_A_UNROLL * g
                base = HALO_A + g0 - (CONV_A_K - 1)
                start = _after(bias, conv_a_tail[-1], slow=len(conv_a_tail) % 2 == 0) if conv_a_tail else bias
                accs = [start] * CONV_A_UNROLL
                for k in range(CONV_A_K):
                    wk = jnp.broadcast_to(caw_ref[k:k + 1, lo:hi], (SUBLANES, LANES))
                    for j in range(CONV_A_UNROLL):
                        accs[j] = accs[j] + shifted(src, base + SUBLANES * j + k, SUBLANES) * wk
                for j in range(CONV_A_UNROLL):
                    tbuf[g0 + SUBLANES * j:g0 + SUBLANES * (j + 1), lo:hi] = accs[j]
                conv_a_tail.append(accs[-1])

    def prep_cd(h):
        n, r0 = OUT_ROWS, OUT_ROWS * h
        low_half = jax.lax.broadcasted_iota(jnp.int32, (1, LANES), 1) < HEAD
        cr = c0 + r0
        s16 = shifted(s8buf, cr, n) + shifted(s8buf, cr - 8, n)
        mean0 = jnp.where(low_half, s2buf[0, cr:cr + n, :] * 0.5, s4buf[0, cr:cr + n, :] * 0.25)
        mean1 = jnp.where(low_half, s8buf[cr:cr + n, :] * 0.125, s16 * 0.0625)
        mean = jnp.concatenate([mean0, mean1], axis=-1)
        if h == 0:
            head_rows = mean[0:HALO_C, :] * jnp.where(first_prev, corr_ref[...], 1.0)
            mean = jnp.concatenate([head_rows, mean[HALO_C:, :]], axis=0)
        ch = jnp.concatenate([shifted(csrc[s], cr, n) for s in range(2)], axis=-1)
        pooled = (mean - ch).astype(BF16)
        v = _layer_norm_rows(kept_r[4, half(h), :], sg_ref[...], sb_ref[...]).astype(BF16)
        return pooled, v

    def mean_a(h):
        return _dot(tbuf[half(h), :].astype(BF16), phead_ref[...])

    def var_a(h, mu):
        d = tbuf[half(h), :] - mu
        tbuf[half(h), :] = d
        return _dot((d * d).astype(BF16), phead_ref[...])

    def dots_cd(pooled, v):
        yc = _dot(pooled, pbd_ref[...])
        head_of_lane = jax.lax.broadcasted_iota(jnp.int32, (1, GROUP), 1) // HEAD
        sp = []
        for q in range(OUT_ROWS // SGU_BLOCK):
            blk = v[SGU_BLOCK * q:SGU_BLOCK * (q + 1), :]
            vbig = jnp.concatenate(
                [jnp.where(head_of_lane == hd, blk, jnp.zeros_like(blk)) for hd in range(4)], axis=0)
            sp.append(_dot(ws_ref[...], vbig) + sbias_ref[...])
        return yc, jnp.concatenate(sp, axis=0)

    def finish(h, var, yc, sp):
        rs = half(h)
        r0, n = OUT_ROWS * h, OUT_ROWS
        an = tbuf[rs, :] * jax.lax.rsqrt(var + LN_EPS) * nag_ref[...] + nab_ref[...]
        mix[rs, 0:GROUP] = (_silu(an) * kept_r[0, rs, :]).astype(BF16)
        conv_b = []
        for s in range(2):
            lo, hi = LANES * s, LANES * (s + 1)
            src = bbuf_r.at[s]
            acc = shifted(src, HALO_B + r0, n) * cbw_ref[CONV_B_K - 1:CONV_B_K, lo:hi]
            for k in range(CONV_B_K - 1):
                acc = acc + shifted(src, HALO_B + r0 - (CONV_B_K - 1) + k, n) * cbw_ref[k:k + 1, lo:hi]
            conv_b.append(acc)
        mix[rs, GROUP:2 * GROUP] = (jnp.concatenate(conv_b, axis=-1) * kept_r[1, rs, :]).astype(BF16)
        mix[rs, 2 * GROUP:3 * GROUP] = (yc * kept_r[2, rs, :]).astype(BF16)
        mix[rs, 3 * GROUP:4 * GROUP] = (sp * kept_r[3, rs, :]).astype(BF16)

    def out_proj(h):
        o_ref[half(h), :] = _dot(mix[half(h), :], wout_ref[...]) + bout_ref[...]

    def post_norm(h):
        for q in range(OUT_ROWS * h, OUT_ROWS * (h + 1), NORM_ROWS):
            rs = slice(q, q + NORM_ROWS)
            r = ALPHA * xprev_ref[rs, :] + o_ref[rs, :]
            y = _layer_norm_rows(r, lng_ref[...], lnb_ref[...])
            o_ref[rs, :] = y
            if ob_ref is not None:
                ob_ref[rs, :] = y.astype(BF16)

    do1, do2 = cur_slot is not None, prev_slot is not None
    state = {}
    schedule = (
        (2, pool_sums), (2, lambda: conv_a(0)), (2, lambda: conv_a(1)),
        (1, proj_a), (1, proj_a_gate), (1, proj_b_gate), (1, proj_b), (1, proj_c),
        (2, lambda: state.update(cd0=prep_cd(0), cd1=prep_cd(1))),
        (2, lambda: state.update(mu0=mean_a(0), mu1=mean_a(1))),
        (1, proj_c_gate),
        (2, lambda: state.update(var0=var_a(0, state["mu0"]), var1=var_a(1, state["mu1"]))),
        (2, lambda: state.update(y0=dots_cd(*state["cd0"]), y1=dots_cd(*state["cd1"]))),
        (1, proj_d),
        (2, lambda: finish(0, state["var0"], *state["y0"])), (2, lambda: out_proj(0)),
        (2, lambda: finish(1, state["var1"], *state["y1"])), (2, lambda: out_proj(1)),
        (2, lambda: post_norm(0)),
        (1, proj_d_gate),
        (2, lambda: post_norm(1)),
    )
    for stage, piece in schedule:
        if (stage == 1 and do1) or (stage == 2 and do2):
            piece()

    if do1:
        if do2:
            fresh = (pid % tiles_per_seq) == 0
            abuf_w[:, 0:HALO_A, :] = jnp.where(fresh, 0.0, abuf_r[:, rows:rows + HALO_A, :])
            bbuf_w[:, 0:HALO_B, :] = jnp.where(fresh, 0.0, bbuf_r[:, rows:rows + HALO_B, :])
            cbuf_w[:, PAD_C:c0, :] = jnp.where(fresh, 0.0, cbuf_r[:, rows + PAD_C:rows + c0, :])
        else:
            abuf_w[:, 0:HALO_A, :] = jnp.zeros((2, HALO_A, LANES), F32)
            bbuf_w[:, 0:HALO_B, :] = jnp.zeros((2, HALO_B, LANES), F32)
            cbuf_w[:, PAD_C:c0, :] = jnp.zeros((2, HALO_C, LANES), F32)


def _layer_param_spec(layer, shape):
    return pl.BlockSpec((None,) + tuple(shape[1:]), lambda i: (layer,) + (0,) * (len(shape) - 1))


def _shared_spec(shape):
    return pl.BlockSpec(shape, lambda i: (0,) * len(shape))


def _layer(x_lhs, x_res, seq_len, layer, n_layers, stacked, shared):
    n = x_res.shape[0]
    assert seq_len % TILE_ROWS == 0 and TILE_ROWS == 2 * OUT_ROWS and OUT_ROWS % SGU_BLOCK == 0
    assert len(stacked) == N_LAYER_PARAMS
    first_layer, last_layer = layer == 0, layer == n_layers - 1
    n_tiles = n // TILE_ROWS
    tile = (TILE_ROWS, D_MODEL)
    in_specs = [pl.BlockSpec(tile, lambda i: (jnp.minimum(i, n_tiles - 1), 0)),
                pl.BlockSpec(tile, lambda i: (jnp.maximum(i - 1, 0), 0))]
    in_specs += [_layer_param_spec(layer, p.shape) for p in stacked]
    in_specs += [_shared_spec(p.shape) for p in shared]
    out_spec = pl.BlockSpec(tile, lambda i: (jnp.maximum(i - 1, 0), 0))
    out_specs, out_shape = [out_spec], [jax.ShapeDtypeStruct((n, D_MODEL), F32)]
    if not last_layer:
        out_specs.append(out_spec)
        out_shape.append(jax.ShapeDtypeStruct((n, D_MODEL), BF16))
    hist = lambda halo: pltpu.VMEM((2, halo + TILE_ROWS, LANES), F32)
    slot = [
        hist(HALO_A),
        hist(HALO_B),
        hist(PAD_C + HALO_C),
        pltpu.VMEM((N_KEPT, TILE_ROWS, GROUP), F32),
    ]
    assert len(slot) == N_SLOT_BUFFERS
    work = lambda *lead: pltpu.VMEM((*lead, PAD_C + HALO_C + TILE_ROWS, LANES), F32)
    scratch = slot + slot + [
        work(2),
        work(2),
        work(),
        pltpu.VMEM((TILE_ROWS, GROUP), F32),
        pltpu.VMEM((TILE_ROWS, D_MODEL), BF16),
        pltpu.VMEM((SGU_BLOCK, 4 * SGU_BLOCK), BF16),
        pltpu.VMEM((D_MODEL, 12 * GROUP), BF16),
        pltpu.VMEM((D_MODEL, D_MODEL), BF16),
    ]
    if first_layer:
        scratch.append(pltpu.VMEM((TILE_ROWS, D_MODEL), BF16))
    outs = pl.pallas_call(
        functools.partial(_layer_kernel, tiles_per_seq=seq_len // TILE_ROWS, n_tiles=n_tiles,
                          first_layer=first_layer, last_layer=last_layer),
        grid=(n_tiles + 1,),
        in_specs=in_specs,
        out_specs=out_specs,
        out_shape=out_shape,
        scratch_shapes=scratch,
        input_output_aliases=({} if first_layer else ({1: 0} if last_layer else {1: 0, 0: 1})),
        compiler_params=pltpu.CompilerParams(
            dimension_semantics=("arbitrary",), vmem_limit_bytes=VMEM_LIMIT_BYTES),
        name="hybrid_layer",
    )(x_lhs, x_res, *stacked, *shared)
    return outs[0], (None if last_layer else outs[1])


def _constants():
    head_id = np.arange(GROUP) // HEAD
    phead = (head_id[:, None] == head_id[None, :]).astype(np.float32) / HEAD
    win = np.asarray(POOL_WINDOWS, np.float32)[head_id]
    pos1 = np.arange(1, HALO_C + 1, dtype=np.float32)[:, None]
    corr = win[None, :] / np.minimum(pos1, win[None, :])
    return jnp.asarray(phead, BF16), jnp.asarray(corr, F32)


def kernel(x, ln_g, ln_b, w_in, b_in, conv_a_w, conv_a_b, norm_a_g, norm_a_b, conv_b_w, pool_w, pool_scale, sgu_ln_g, sgu_ln_b, sgu_w, sgu_bias, w_out, b_out):
    bsz, seq_len, d = x.shape
    depth = w_in.shape[0]
    phead, corr = _constants()
    rows = lambda v: v.reshape(depth, 1, -1)
    pool_bd = jnp.zeros((depth, GROUP, GROUP), F32)
    for g in range(len(POOL_WINDOWS)):
        pool_bd = pool_bd.at[:, HEAD * g:HEAD * (g + 1), HEAD * g:HEAD * (g + 1)].set(pool_w[:, g])
    sgu_wide = jnp.transpose(sgu_w, (0, 2, 1, 3)).reshape(depth, SGU_BLOCK, 4 * SGU_BLOCK)
    sgu_bias_tbl = jnp.repeat(jnp.transpose(sgu_bias, (0, 2, 1)), HEAD, axis=2)
    stacked = (
        w_in, rows(b_in),
        conv_a_w, rows(conv_a_b), rows(norm_a_g), rows(norm_a_b),
        conv_b_w,
        pool_bd.astype(BF16), rows(pool_scale),
        rows(sgu_ln_g), rows(sgu_ln_b), sgu_wide, sgu_bias_tbl,
        w_out, rows(b_out), rows(ln_g), rows(ln_b),
    )
    shared = (corr, phead)
    h = x.reshape(bsz * seq_len, d)
    h_bf16 = None
    for layer in range(depth):
        h, h_bf16 = _layer(h if h_bf16 is None else h_bf16, h, seq_len, layer, depth, stacked, shared)
    return h.reshape(bsz, seq_len, d)
```

```python
import functools

import jax
import jax.numpy as jnp
import numpy as np
from jax.experimental import pallas as pl
from jax.experimental.pallas import tpu as pltpu

D_MODEL = 1024
DEPTH = 4
GROUP = 256
HEAD = 64
CONV_A_K = 31
CONV_B_K = 3
POOL_WINDOWS = (2, 4, 8, 16)
SGU_BLOCK = 128
CHUNK = 64
LN_EPS = 1e-5
ALPHA = float((2.0 * DEPTH) ** 0.25)

LANES = 128
SUBLANES = 8
TILE_ROWS = 512
NORM_ROWS = SGU_BLOCK
OUT_ROWS = 256
HALO_A = 32
GATE_ROWS = SUBLANES
HALO_B = 8
HALO_C = 16
PAD_C = 8
N_KEPT = 5
WEIGHT_CAST_ROWS = 128
CONV_A_UNROLL = 4
VMEM_LIMIT_BYTES = 56 * 1024 * 1024
N_LAYER_PARAMS = 17
N_SLOT_BUFFERS = 4

F32 = jnp.float32
BF16 = jnp.bfloat16


def _silu(v):
    return v * jax.nn.sigmoid(v)


def _dot(a, b):
    return jnp.dot(a, b, preferred_element_type=F32)


def _after(value, anchor, slow=False):
    bits = jax.lax.bitcast_convert_type(anchor, jnp.uint32)
    zero = jax.lax.shift_right_logical(jax.lax.shift_right_logical(bits, jnp.uint32(16)), jnp.uint32(16))
    if slow:
        total = jnp.sum(jax.lax.bitcast_convert_type(zero, F32), axis=-1, keepdims=True)
        zero = jax.lax.bitcast_convert_type(jnp.broadcast_to(total, value.shape), jnp.uint32)
    return jax.lax.bitcast_convert_type(jax.lax.bitcast_convert_type(value, jnp.uint32) | zero, F32)


def _layer_norm_rows(v, g, b):
    mu = jnp.mean(v, axis=-1, keepdims=True)
    d = v - mu
    var = jnp.mean(d * d, axis=-1, keepdims=True)
    return d * jax.lax.rsqrt(var + LN_EPS) * g + b


def _layer_kernel(*refs, tiles_per_seq, n_tiles):
    x_ref, xprev_ref = refs[0:2]
    params = refs[2:2 + N_LAYER_PARAMS]
    corr_ref, phead_ref, o_ref = refs[2 + N_LAYER_PARAMS:5 + N_LAYER_PARAMS]
    scratch = refs[5 + N_LAYER_PARAMS:]
    slots = (scratch[0:N_SLOT_BUFFERS], scratch[N_SLOT_BUFFERS:2 * N_SLOT_BUFFERS])
    s2buf, s4buf, s8buf, tbuf, mix, xb_ref, ws_ref, win_bf, wout_bf = scratch[2 * N_SLOT_BUFFERS:]
    win_ref, swb_ref, wout_ref = params[0], params[11], params[13]
    pid = pl.program_id(0)

    step = functools.partial(
        _pipeline_step, pid, tiles_per_seq, x_ref, xprev_ref, params, corr_ref, phead_ref, o_ref,
        s2buf, s4buf, s8buf, tbuf, mix, xb_ref, ws_ref, win_bf, wout_bf)

    @pl.when(pid == 0)
    def _first():
        def cast_rows(k, carry):
            rs = pl.ds(pl.multiple_of(k * WEIGHT_CAST_ROWS, WEIGHT_CAST_ROWS), WEIGHT_CAST_ROWS)
            win_bf[rs, :] = win_ref[rs, :].astype(BF16)
            wout_bf[rs, :] = wout_ref[rs, :].astype(BF16)
            return carry
        jax.lax.fori_loop(0, D_MODEL // WEIGHT_CAST_ROWS, cast_rows, 0)
        for slot in slots:
            slot[2][:, 0:PAD_C, :] = jnp.zeros((2, PAD_C, LANES), F32)
        s2buf[:, 0:PAD_C, :] = jnp.zeros((2, PAD_C, LANES), F32)
        wi = jax.lax.broadcasted_iota(jnp.int32, (SGU_BLOCK, 4 * SGU_BLOCK), 0)
        wj = jax.lax.broadcasted_iota(jnp.int32, (SGU_BLOCK, 4 * SGU_BLOCK), 1) % SGU_BLOCK
        ws_ref[...] = jnp.where(wj // CHUNK <= wi // CHUNK, swb_ref[...], 0.0).astype(BF16)
        step(slots[0], None)

    for parity in range(2):
        @pl.when((pid > 0) & (pid < n_tiles) & (pid % 2 == parity))
        def _middle(parity=parity):
            step(slots[parity], slots[1 - parity])

    @pl.when(pid == n_tiles)
    def _last():
        step(None, slots[(n_tiles - 1) % 2])


def _pipeline_step(pid, tiles_per_seq, x_ref, xprev_ref, params, corr_ref, phead_ref, o_ref,
                   s2buf, s4buf, s8buf, tbuf, mix, xb_ref, ws_ref, win_ref, wout_ref,
                   cur_slot, prev_slot):
    (_, bin_ref, caw_ref, cab_ref, nag_ref, nab_ref, cbw_ref, pbd_ref, psc_ref,
     sg_ref, sb_ref, _, sbias_ref, _, bout_ref, lng_ref, lnb_ref) = params
    rows = TILE_ROWS
    dz = jnp.minimum(pid, 0)
    c0 = PAD_C + HALO_C

    def shifted(ref, start, n):
        return ref[pl.ds(dz + start, n), :]

    if cur_slot is not None:
        abuf_w, bbuf_w, cbuf_w, kept_w = cur_slot
        xb_ref[...] = x_ref[...].astype(BF16)

    def proj(i):
        lo, hi = GROUP * i, GROUP * (i + 1)
        return _dot(xb_ref[...], win_ref[:, lo:hi]) + bin_ref[:, lo:hi]

    def put_slabs(buf, halo, val):
        for s in range(2):
            buf[s, halo:halo + rows, :] = val[:, LANES * s:LANES * (s + 1)]

    def proj_a():
        put_slabs(abuf_w, HALO_A, proj(0) * jax.nn.sigmoid(proj(1)))

    def proj_a_gate():
        kept_w[0] = _silu(proj(2))

    def proj_b_gate():
        kept_w[1] = proj(3) * _silu(proj(6))

    def proj_b():
        put_slabs(bbuf_w, HALO_B, proj(4) * proj(5))

    def proj_c():
        put_slabs(cbuf_w, c0, proj(7))

    def proj_c_gate():
        kept_w[2] = _silu(proj(8)) * psc_ref[...]

    def proj_d_gate():
        kept_w[3] = proj(9) * _silu(proj(11))

    def proj_d():
        kept_w[4] = proj(10)

    if prev_slot is not None:
        abuf_r, bbuf_r, cbuf_r, kept_r = prev_slot
        first_prev = ((pid - 1) % tiles_per_seq) == 0
        csrc = [cbuf_r.at[s] for s in range(2)]

    def pool_sums():
        n2 = rows + HALO_C
        for s in range(2):
            s2buf[s, PAD_C:PAD_C + n2, :] = shifted(csrc[s], PAD_C, n2) + shifted(csrc[s], PAD_C - 1, n2)
        for s in range(2):
            s4buf[s, PAD_C:PAD_C + n2, :] = (
                shifted(s2buf.at[s], PAD_C, n2) + shifted(s2buf.at[s], PAD_C - 2, n2))
        n8 = rows + 8
        s8buf[c0 - 8:c0 - 8 + n8, :] = shifted(s4buf.at[1], c0 - 8, n8) + shifted(s4buf.at[1], c0 - 12, n8)

    def half(h):
        return slice(OUT_ROWS * h, OUT_ROWS * (h + 1))

    conv_a_tail = []

    def conv_a(h):
        for s in range(2):
            lo, hi = LANES * s, LANES * (s + 1)
            src = abuf_r.at[s]
            bias = jnp.broadcast_to(cab_ref[:, lo:hi], (SUBLANES, LANES))
            for g in range(OUT_ROWS // (SUBLANES * CONV_A_UNROLL)):
                g0 = OUT_ROWS * h + SUBLANES * CONV_A_UNROLL * g
                base = HALO_A + g0 - (CONV_A_K - 1)
                start = bias
                if conv_a_tail:
                    start = _after(bias, conv_a_tail[-1], slow=len(conv_a_tail) % 2 == 0)
                    src[HALO_A + rows:HALO_A + rows + GATE_ROWS, :] = conv_a_tail[-1]
                accs = [start] * CONV_A_UNROLL
                for k in range(CONV_A_K):
                    wk = jnp.broadcast_to(caw_ref[k:k + 1, lo:hi], (SUBLANES, LANES))
                    for j in range(CONV_A_UNROLL):
                        accs[j] = accs[j] + shifted(src, base + SUBLANES * j + k, SUBLANES) * wk
                for j in range(CONV_A_UNROLL):
                    tbuf[g0 + SUBLANES * j:g0 + SUBLANES * (j + 1), lo:hi] = accs[j]
                conv_a_tail.append(accs[-1])

    def prep_cd(h):
        n, r0 = OUT_ROWS, OUT_ROWS * h
        low_half = jax.lax.broadcasted_iota(jnp.int32, (1, LANES), 1) < HEAD
        cr = c0 + r0
        s16 = shifted(s8buf, cr, n) + shifted(s8buf, cr - 8, n)
        mean0 = jnp.where(low_half, s2buf[0, cr:cr + n, :] * 0.5, s4buf[0, cr:cr + n, :] * 0.25)
        mean1 = jnp.where(low_half, s8buf[cr:cr + n, :] * 0.125, s16 * 0.0625)
        mean = jnp.concatenate([mean0, mean1], axis=-1)
        if h == 0:
            head_rows = mean[0:HALO_C, :] * jnp.where(first_prev, corr_ref[...], 1.0)
            mean = jnp.concatenate([head_rows, mean[HALO_C:, :]], axis=0)
        ch = jnp.concatenate([shifted(csrc[s], cr, n) for s in range(2)], axis=-1)
        pooled = (mean - ch).astype(BF16)
        v = _layer_norm_rows(kept_r[4, half(h), :], sg_ref[...], sb_ref[...]).astype(BF16)
        return pooled, v

    def mean_a(h):
        return _dot(tbuf[half(h), :].astype(BF16), phead_ref[...])

    def var_a(h, mu):
        d = tbuf[half(h), :] - mu
        tbuf[half(h), :] = d
        return _dot((d * d).astype(BF16), phead_ref[...])

    def dots_cd(pooled, v):
        yc = _dot(pooled, pbd_ref[...])
        head_of_lane = jax.lax.broadcasted_iota(jnp.int32, (1, GROUP), 1) // HEAD
        sp = []
        for q in range(OUT_ROWS // SGU_BLOCK):
            blk = v[SGU_BLOCK * q:SGU_BLOCK * (q + 1), :]
            vbig = jnp.concatenate(
                [jnp.where(head_of_lane == hd, blk, jnp.zeros_like(blk)) for hd in range(4)], axis=0)
            sp.append(_dot(ws_ref[...], vbig) + sbias_ref[...])
        return yc, jnp.concatenate(sp, axis=0)

    def finish(h, var, yc, sp):
        rs = half(h)
        r0, n = OUT_ROWS * h, OUT_ROWS
        an = tbuf[rs, :] * jax.lax.rsqrt(var + LN_EPS) * nag_ref[...] + nab_ref[...]
        mix[rs, 0:GROUP] = (_silu(an) * kept_r[0, rs, :]).astype(BF16)
        conv_b = []
        for s in range(2):
            lo, hi = LANES * s, LANES * (s + 1)
            src = bbuf_r.at[s]
            acc = shifted(src, HALO_B + r0, n) * cbw_ref[CONV_B_K - 1:CONV_B_K, lo:hi]
            for k in range(CONV_B_K - 1):
                acc = acc + shifted(src, HALO_B + r0 - (CONV_B_K - 1) + k, n) * cbw_ref[k:k + 1, lo:hi]
            conv_b.append(acc)
        mix[rs, GROUP:2 * GROUP] = (jnp.concatenate(conv_b, axis=-1) * kept_r[1, rs, :]).astype(BF16)
        mix[rs, 2 * GROUP:3 * GROUP] = (yc * kept_r[2, rs, :]).astype(BF16)
        mix[rs, 3 * GROUP:4 * GROUP] = (sp * kept_r[3, rs, :]).astype(BF16)

    def out_proj(h):
        o_ref[half(h), :] = _dot(mix[half(h), :], wout_ref[...]) + bout_ref[...]

    def post_norm(h):
        for q in range(OUT_ROWS * h, OUT_ROWS * (h + 1), NORM_ROWS):
            rs = slice(q, q + NORM_ROWS)
            r = ALPHA * xprev_ref[rs, :] + o_ref[rs, :]
            o_ref[rs, :] = _layer_norm_rows(r, lng_ref[...], lnb_ref[...])

    do1, do2 = cur_slot is not None, prev_slot is not None
    state = {}
    schedule = (
        (2, pool_sums), (2, lambda: conv_a(0)), (2, lambda: conv_a(1)),
        (1, proj_a), (1, proj_a_gate), (1, proj_b_gate), (1, proj_b), (1, proj_c),
        (2, lambda: state.update(cd0=prep_cd(0), cd1=prep_cd(1))),
        (2, lambda: state.update(mu0=mean_a(0), mu1=mean_a(1))),
        (1, proj_c_gate),
        (2, lambda: state.update(var0=var_a(0, state["mu0"]), var1=var_a(1, state["mu1"]))),
        (2, lambda: state.update(y0=dots_cd(*state["cd0"]), y1=dots_cd(*state["cd1"]))),
        (1, proj_d),
        (2, lambda: finish(0, state["var0"], *state["y0"])), (2, lambda: out_proj(0)),
        (2, lambda: finish(1, state["var1"], *state["y1"])), (2, lambda: out_proj(1)),
        (2, lambda: post_norm(0)),
        (1, proj_d_gate),
        (2, lambda: post_norm(1)),
    )
    for stage, piece in schedule:
        if (stage == 1 and do1) or (stage == 2 and do2):
            piece()

    if do1:
        if do2:
            fresh = (pid % tiles_per_seq) == 0
            abuf_w[:, 0:HALO_A, :] = jnp.where(fresh, 0.0, abuf_r[:, rows:rows + HALO_A, :])
            bbuf_w[:, 0:HALO_B, :] = jnp.where(fresh, 0.0, bbuf_r[:, rows:rows + HALO_B, :])
            cbuf_w[:, PAD_C:c0, :] = jnp.where(fresh, 0.0, cbuf_r[:, rows + PAD_C:rows + c0, :])
        else:
            abuf_w[:, 0:HALO_A, :] = jnp.zeros((2, HALO_A, LANES), F32)
            bbuf_w[:, 0:HALO_B, :] = jnp.zeros((2, HALO_B, LANES), F32)
            cbuf_w[:, PAD_C:c0, :] = jnp.zeros((2, HALO_C, LANES), F32)


def _layer_param_spec(layer, shape):
    return pl.BlockSpec((None,) + tuple(shape[1:]), lambda i: (layer,) + (0,) * (len(shape) - 1))


def _shared_spec(shape):
    return pl.BlockSpec(shape, lambda i: (0,) * len(shape))


def _layer(x2, seq_len, layer, stacked, shared):
    n = x2.shape[0]
    assert seq_len % TILE_ROWS == 0 and TILE_ROWS == 2 * OUT_ROWS and OUT_ROWS % SGU_BLOCK == 0
    assert len(stacked) == N_LAYER_PARAMS
    n_tiles = n // TILE_ROWS
    tile = (TILE_ROWS, D_MODEL)
    in_specs = [pl.BlockSpec(tile, lambda i: (jnp.minimum(i, n_tiles - 1), 0)),
                pl.BlockSpec(tile, lambda i: (jnp.maximum(i - 1, 0), 0))]
    in_specs += [_layer_param_spec(layer, p.shape) for p in stacked]
    in_specs += [_shared_spec(p.shape) for p in shared]
    slot = [
        pltpu.VMEM((2, HALO_A + TILE_ROWS + GATE_ROWS, LANES), F32),
        pltpu.VMEM((2, HALO_B + TILE_ROWS, LANES), F32),
        pltpu.VMEM((2, PAD_C + HALO_C + TILE_ROWS, LANES), F32),
        pltpu.VMEM((N_KEPT, TILE_ROWS, GROUP), F32),
    ]
    assert len(slot) == N_SLOT_BUFFERS
    work = lambda *lead: pltpu.VMEM((*lead, PAD_C + HALO_C + TILE_ROWS, LANES), F32)
    return pl.pallas_call(
        functools.partial(_layer_kernel, tiles_per_seq=seq_len // TILE_ROWS, n_tiles=n_tiles),
        grid=(n_tiles + 1,),
        in_specs=in_specs,
        out_specs=pl.BlockSpec(tile, lambda i: (jnp.maximum(i - 1, 0), 0)),
        out_shape=jax.ShapeDtypeStruct((n, D_MODEL), F32),
        scratch_shapes=slot + slot + [
            work(2),
            work(2),
            work(),
            pltpu.VMEM((TILE_ROWS, GROUP), F32),
            pltpu.VMEM((TILE_ROWS, D_MODEL), BF16),
            pltpu.VMEM((TILE_ROWS, D_MODEL), BF16),
            pltpu.VMEM((SGU_BLOCK, 4 * SGU_BLOCK), BF16),
            pltpu.VMEM((D_MODEL, 12 * GROUP), BF16),
            pltpu.VMEM((D_MODEL, D_MODEL), BF16),
        ],
        compiler_params=pltpu.CompilerParams(
            dimension_semantics=("arbitrary",), vmem_limit_bytes=VMEM_LIMIT_BYTES),
        name="hybrid_layer",
    )(x2, x2, *stacked, *shared)


def _constants():
    head_id = np.arange(GROUP) // HEAD
    phead = (head_id[:, None] == head_id[None, :]).astype(np.float32) / HEAD
    win = np.asarray(POOL_WINDOWS, np.float32)[head_id]
    pos1 = np.arange(1, HALO_C + 1, dtype=np.float32)[:, None]
    corr = win[None, :] / np.minimum(pos1, win[None, :])
    return jnp.asarray(phead, BF16), jnp.asarray(corr, F32)


def kernel(x, ln_g, ln_b, w_in, b_in, conv_a_w, conv_a_b, norm_a_g, norm_a_b, conv_b_w, pool_w, pool_scale, sgu_ln_g, sgu_ln_b, sgu_w, sgu_bias, w_out, b_out):
    bsz, seq_len, d = x.shape
    depth = w_in.shape[0]
    phead, corr = _constants()
    rows = lambda v: v.reshape(depth, 1, -1)
    pool_bd = jnp.zeros((depth, GROUP, GROUP), F32)
    for g in range(len(POOL_WINDOWS)):
        pool_bd = pool_bd.at[:, HEAD * g:HEAD * (g + 1), HEAD * g:HEAD * (g + 1)].set(pool_w[:, g])
    sgu_wide = jnp.transpose(sgu_w, (0, 2, 1, 3)).reshape(depth, SGU_BLOCK, 4 * SGU_BLOCK)
    sgu_bias_tbl = jnp.repeat(jnp.transpose(sgu_bias, (0, 2, 1)), HEAD, axis=2)
    stacked = (
        w_in, rows(b_in),
        conv_a_w, rows(conv_a_b), rows(norm_a_g), rows(norm_a_b),
        conv_b_w,
        pool_bd.astype(BF16), rows(pool_scale),
        rows(sgu_ln_g), rows(sgu_ln_b), sgu_wide, sgu_bias_tbl,
        w_out, rows(b_out), rows(ln_g), rows(ln_b),
    )
    shared = (corr, phead)
    h = x.reshape(bsz * seq_len, d)
    for layer in range(depth):
        h = _layer(h, seq_len, layer, stacked, shared)
    return h.reshape(bsz, seq_len, d)
```

```python
import functools

import jax
import jax.numpy as jnp
import numpy as np
from jax.experimental import pallas as pl
from jax.experimental.pallas import tpu as pltpu

D_MODEL = 1024
DEPTH = 4
GROUP = 256
HEAD = 64
CONV_A_K = 31
CONV_B_K = 3
POOL_WINDOWS = (2, 4, 8, 16)
SGU_BLOCK = 128
CHUNK = 64
LN_EPS = 1e-5
ALPHA = float((2.0 * DEPTH) ** 0.25)

LANES = 128
SUBLANES = 8
TILE_ROWS = 512
NORM_ROWS = SGU_BLOCK
OUT_ROWS = 256
HALO_A = 32
HALO_B = 32
HALO_C = 16
PAD_C = 16
N_KEPT = 5
WEIGHT_CAST_ROWS = 128
CONV_A_UNROLL = 4
VMEM_LIMIT_BYTES = 56 * 1024 * 1024
N_LAYER_PARAMS = 7
SMALL_ROWS = 16
CONV_B_ROW = 8
N_SLOT_BUFFERS = 4

F32 = jnp.float32
BF16 = jnp.bfloat16


def _silu(v):
    return v * jax.nn.sigmoid(v)


def _dot(a, b):
    return jnp.dot(a, b, preferred_element_type=F32)


def _after(value, anchor, slow=False):
    bits = jax.lax.bitcast_convert_type(anchor, jnp.uint32)
    zero = jax.lax.shift_right_logical(jax.lax.shift_right_logical(bits, jnp.uint32(16)), jnp.uint32(16))
    if slow:
        total = jnp.sum(jax.lax.bitcast_convert_type(zero, F32), axis=-1, keepdims=True)
        zero = jax.lax.bitcast_convert_type(jnp.broadcast_to(total, value.shape), jnp.uint32)
    return jax.lax.bitcast_convert_type(jax.lax.bitcast_convert_type(value, jnp.uint32) | zero, F32)


def _layer_norm_rows(v, g, b):
    mu = jnp.mean(v, axis=-1, keepdims=True)
    d = v - mu
    var = jnp.mean(d * d, axis=-1, keepdims=True)
    return d * jax.lax.rsqrt(var + LN_EPS) * g + b


def _layer_kernel(*refs, tiles_per_seq, n_tiles):
    x_ref, xprev_ref = refs[0:2]
    params = refs[2:2 + N_LAYER_PARAMS]
    corr_ref, phead_ref, o_ref = refs[2 + N_LAYER_PARAMS:5 + N_LAYER_PARAMS]
    scratch = refs[5 + N_LAYER_PARAMS:]
    slots = (scratch[0:N_SLOT_BUFFERS], scratch[N_SLOT_BUFFERS:2 * N_SLOT_BUFFERS])
    s2buf, s4buf, s8buf, tbuf, mix, xb_ref, ws_ref, win_bf, wout_bf = scratch[2 * N_SLOT_BUFFERS:]
    win_ref, swb_ref, wout_ref = params[0], params[4], params[6]
    pid = pl.program_id(0)

    step = functools.partial(
        _pipeline_step, pid, tiles_per_seq, x_ref, xprev_ref, params, corr_ref, phead_ref, o_ref,
        s2buf, s4buf, s8buf, tbuf, mix, xb_ref, ws_ref, win_bf, wout_bf)

    @pl.when(pid == 0)
    def _first():
        def cast_rows(k, carry):
            rs = pl.ds(pl.multiple_of(k * WEIGHT_CAST_ROWS, WEIGHT_CAST_ROWS), WEIGHT_CAST_ROWS)
            win_bf[rs, :] = win_ref[rs, :].astype(BF16)
            wout_bf[rs, :] = wout_ref[rs, :].astype(BF16)
            return carry
        jax.lax.fori_loop(0, D_MODEL // WEIGHT_CAST_ROWS, cast_rows, 0)
        for slot in slots:
            slot[2][:, 0:PAD_C, :] = jnp.zeros((2, PAD_C, LANES), F32)
        s2buf[:, 0:PAD_C, :] = jnp.zeros((2, PAD_C, LANES), F32)
        wi = jax.lax.broadcasted_iota(jnp.int32, (SGU_BLOCK, 4 * SGU_BLOCK), 0)
        wj = jax.lax.broadcasted_iota(jnp.int32, (SGU_BLOCK, 4 * SGU_BLOCK), 1) % SGU_BLOCK
        ws_ref[...] = jnp.where(wj // CHUNK <= wi // CHUNK, swb_ref[...], 0.0).astype(BF16)
        step(slots[0], None)

    for parity in range(2):
        @pl.when((pid > 0) & (pid < n_tiles) & (pid % 2 == parity))
        def _middle(parity=parity):
            step(slots[parity], slots[1 - parity])

    @pl.when(pid == n_tiles)
    def _last():
        step(None, slots[(n_tiles - 1) % 2])


def _pipeline_step(pid, tiles_per_seq, x_ref, xprev_ref, params, corr_ref, phead_ref, o_ref,
                   s2buf, s4buf, s8buf, tbuf, mix, xb_ref, ws_ref, win_ref, wout_ref,
                   cur_slot, prev_slot):
    _, small_ref, caw_ref, pbd_ref, _, sbias_ref, _ = params
    bin_ref = small_ref.at[0:1, :]
    cab_ref, nag_ref, nab_ref, psc_ref, sg_ref, sb_ref = (
        small_ref.at[1:2, GROUP * i:GROUP * (i + 1)] for i in range(6))
    bout_ref, lng_ref, lnb_ref = (small_ref.at[2:3, D_MODEL * i:D_MODEL * (i + 1)] for i in range(3))
    cbw_ref = small_ref.at[CONV_B_ROW:CONV_B_ROW + CONV_B_K, 0:GROUP]
    rows = TILE_ROWS
    dz = jnp.minimum(pid, 0)
    c0 = PAD_C + HALO_C

    def shifted(ref, start, n):
        return ref[pl.ds(dz + start, n), :]

    if cur_slot is not None:
        abuf_w, bbuf_w, cbuf_w, kept_w = cur_slot
        xb_ref[...] = x_ref[...].astype(BF16)

    def proj(i):
        lo, hi = GROUP * i, GROUP * (i + 1)
        return _dot(xb_ref[...], win_ref[:, lo:hi]) + bin_ref[:, lo:hi]

    def put_slabs(buf, halo, val):
        for s in range(2):
            buf[s, halo:halo + rows, :] = val[:, LANES * s:LANES * (s + 1)]

    def proj_a():
        put_slabs(abuf_w, HALO_A, proj(0) * jax.nn.sigmoid(proj(1)))

    def proj_a_gate():
        kept_w[0] = _silu(proj(2))

    def proj_b_gate():
        kept_w[1] = proj(3) * _silu(proj(6))

    def proj_b():
        put_slabs(bbuf_w, HALO_B, proj(4) * proj(5))

    def proj_c():
        put_slabs(cbuf_w, c0, proj(7))

    def proj_c_gate():
        kept_w[2] = _silu(proj(8)) * psc_ref[...]

    def proj_d_gate():
        kept_w[3] = proj(9) * _silu(proj(11))

    def proj_d():
        kept_w[4] = proj(10)

    if prev_slot is not None:
        abuf_r, bbuf_r, cbuf_r, kept_r = prev_slot
        first_prev = ((pid - 1) % tiles_per_seq) == 0
        csrc = [cbuf_r.at[s] for s in range(2)]

    def pool_sums():
        n2 = rows + HALO_C
        for s in range(2):
            s2buf[s, PAD_C:PAD_C + n2, :] = shifted(csrc[s], PAD_C, n2) + shifted(csrc[s], PAD_C - 1, n2)
        for s in range(2):
            s4buf[s, PAD_C:PAD_C + n2, :] = (
                shifted(s2buf.at[s], PAD_C, n2) + shifted(s2buf.at[s], PAD_C - 2, n2))
        n8 = rows + 8
        s8buf[c0 - 8:c0 - 8 + n8, :] = shifted(s4buf.at[1], c0 - 8, n8) + shifted(s4buf.at[1], c0 - 12, n8)

    def half(h):
        return slice(OUT_ROWS * h, OUT_ROWS * (h + 1))

    conv_a_tail = []

    def conv_a(h):
        for s in range(2):
            lo, hi = LANES * s, LANES * (s + 1)
            src = abuf_r.at[s]
            bias = jnp.broadcast_to(cab_ref[:, lo:hi], (SUBLANES, LANES))
            for g in range(OUT_ROWS // (SUBLANES * CONV_A_UNROLL)):
                g0 = OUT_ROWS * h + SUBLANES * CONV_A_UNROLL * g
                base = HALO_A + g0 - (CONV_A_K - 1)
                start = _after(bias, conv_a_tail[-1], slow=len(conv_a_tail) % 2 == 0) if conv_a_tail else bias
                accs = [start] * CONV_A_UNROLL
                for k in range(CONV_A_K):
                    wk = jnp.broadcast_to(caw_ref[k:k + 1, lo:hi], (SUBLANES, LANES))
                    for j in range(CONV_A_UNROLL):
                        accs[j] = accs[j] + shifted(src, base + SUBLANES * j + k, SUBLANES) * wk
                for j in range(CONV_A_UNROLL):
                    tbuf[g0 + SUBLANES * j:g0 + SUBLANES * (j + 1), lo:hi] = accs[j]
                conv_a_tail.append(accs[-1])

    def prep_cd(h):
        n, r0 = OUT_ROWS, OUT_ROWS * h
        low_half = jax.lax.broadcasted_iota(jnp.int32, (1, LANES), 1) < HEAD
        cr = c0 + r0
        s16 = shifted(s8buf, cr, n) + shifted(s8buf, cr - 8, n)
        mean0 = jnp.where(low_half, s2buf[0, cr:cr + n, :] * 0.5, s4buf[0, cr:cr + n, :] * 0.25)
        mean1 = jnp.where(low_half, s8buf[cr:cr + n, :] * 0.125, s16 * 0.0625)
        mean = jnp.concatenate([mean0, mean1], axis=-1)
        if h == 0:
            head_rows = mean[0:HALO_C, :] * jnp.where(first_prev, corr_ref[...], 1.0)
            mean = jnp.concatenate([head_rows, mean[HALO_C:, :]], axis=0)
        ch = jnp.concatenate([shifted(csrc[s], cr, n) for s in range(2)], axis=-1)
        pooled = (mean - ch).astype(BF16)
        v = _layer_norm_rows(kept_r[4, half(h), :], sg_ref[...], sb_ref[...]).astype(BF16)
        return pooled, v

    def mean_a(h):
        return _dot(tbuf[half(h), :].astype(BF16), phead_ref[...])

    def var_a(h, mu):
        d = tbuf[half(h), :] - mu
        tbuf[half(h), :] = d
        return _dot((d * d).astype(BF16), phead_ref[...])

    def dots_cd(pooled, v):
        yc = _dot(pooled, pbd_ref[...])
        head_of_lane = jax.lax.broadcasted_iota(jnp.int32, (1, GROUP), 1) // HEAD
        sp = []
        for q in range(OUT_ROWS // SGU_BLOCK):
            blk = v[SGU_BLOCK * q:SGU_BLOCK * (q + 1), :]
            vbig = jnp.concatenate(
                [jnp.where(head_of_lane == hd, blk, jnp.zeros_like(blk)) for hd in range(4)], axis=0)
            sp.append(_dot(ws_ref[...], vbig) + sbias_ref[...])
        return yc, jnp.concatenate(sp, axis=0)

    def finish(h, var, yc, sp):
        rs = half(h)
        r0, n = OUT_ROWS * h, OUT_ROWS
        an = tbuf[rs, :] * jax.lax.rsqrt(var + LN_EPS) * nag_ref[...] + nab_ref[...]
        mix[rs, 0:GROUP] = (_silu(an) * kept_r[0, rs, :]).astype(BF16)
        conv_b = []
        for s in range(2):
            lo, hi = LANES * s, LANES * (s + 1)
            src = bbuf_r.at[s]
            acc = shifted(src, HALO_B + r0, n) * cbw_ref[CONV_B_K - 1:CONV_B_K, lo:hi]
            for k in range(CONV_B_K - 1):
                acc = acc + shifted(src, HALO_B + r0 - (CONV_B_K - 1) + k, n) * cbw_ref[k:k + 1, lo:hi]
            conv_b.append(acc)
        mix[rs, GROUP:2 * GROUP] = (jnp.concatenate(conv_b, axis=-1) * kept_r[1, rs, :]).astype(BF16)
        mix[rs, 2 * GROUP:3 * GROUP] = (yc * kept_r[2, rs, :]).astype(BF16)
        mix[rs, 3 * GROUP:4 * GROUP] = (sp * kept_r[3, rs, :]).astype(BF16)

    def out_proj(h):
        o_ref[half(h), :] = _dot(mix[half(h), :], wout_ref[...]) + bout_ref[...]

    def post_norm(h):
        for q in range(OUT_ROWS * h, OUT_ROWS * (h + 1), NORM_ROWS):
            rs = slice(q, q + NORM_ROWS)
            r = ALPHA * xprev_ref[rs, :] + o_ref[rs, :]
            o_ref[rs, :] = _layer_norm_rows(r, lng_ref[...], lnb_ref[...])

    do1, do2 = cur_slot is not None, prev_slot is not None
    state = {}
    schedule = (
        (2, pool_sums), (2, lambda: conv_a(0)), (2, lambda: conv_a(1)),
        (1, proj_a), (1, proj_a_gate), (1, proj_b_gate), (1, proj_b), (1, proj_c),
        (2, lambda: state.update(cd0=prep_cd(0), cd1=prep_cd(1))),
        (2, lambda: state.update(mu0=mean_a(0), mu1=mean_a(1))),
        (1, proj_c_gate),
        (2, lambda: state.update(var0=var_a(0, state["mu0"]), var1=var_a(1, state["mu1"]))),
        (2, lambda: state.update(y0=dots_cd(*state["cd0"]), y1=dots_cd(*state["cd1"]))),
        (1, proj_d),
        (2, lambda: finish(0, state["var0"], *state["y0"])), (2, lambda: out_proj(0)),
        (2, lambda: finish(1, state["var1"], *state["y1"])), (2, lambda: out_proj(1)),
        (2, lambda: post_norm(0)),
        (1, proj_d_gate),
        (2, lambda: post_norm(1)),
    )
    for stage, piece in schedule:
        if (stage == 1 and do1) or (stage == 2 and do2):
            piece()

    if do1:
        if do2:
            fresh = (pid % tiles_per_seq) == 0
            abuf_w[:, 0:HALO_A, :] = jnp.where(fresh, 0.0, abuf_r[:, rows:rows + HALO_A, :])
            bbuf_w[:, 0:HALO_B, :] = jnp.where(fresh, 0.0, bbuf_r[:, rows:rows + HALO_B, :])
            cbuf_w[:, PAD_C:c0, :] = jnp.where(fresh, 0.0, cbuf_r[:, rows + PAD_C:rows + c0, :])
        else:
            abuf_w[:, 0:HALO_A, :] = jnp.zeros((2, HALO_A, LANES), F32)
            bbuf_w[:, 0:HALO_B, :] = jnp.zeros((2, HALO_B, LANES), F32)
            cbuf_w[:, PAD_C:c0, :] = jnp.zeros((2, HALO_C, LANES), F32)


def _layer_param_spec(layer, shape):
    return pl.BlockSpec((None,) + tuple(shape[1:]), lambda i: (layer,) + (0,) * (len(shape) - 1))


def _shared_spec(shape):
    return pl.BlockSpec(shape, lambda i: (0,) * len(shape))


def _layer(x2, seq_len, layer, stacked, shared):
    n = x2.shape[0]
    assert seq_len % TILE_ROWS == 0 and TILE_ROWS == 2 * OUT_ROWS and OUT_ROWS % SGU_BLOCK == 0
    assert len(stacked) == N_LAYER_PARAMS
    n_tiles = n // TILE_ROWS
    tile = (TILE_ROWS, D_MODEL)
    in_specs = [pl.BlockSpec(tile, lambda i: (jnp.minimum(i, n_tiles - 1), 0)),
                pl.BlockSpec(tile, lambda i: (jnp.maximum(i - 1, 0), 0))]
    in_specs += [_layer_param_spec(layer, p.shape) for p in stacked]
    in_specs += [_shared_spec(p.shape) for p in shared]
    slot = [
        pltpu.VMEM((2, HALO_A + TILE_ROWS, LANES), F32),
        pltpu.VMEM((2, HALO_B + TILE_ROWS, LANES), F32),
        pltpu.VMEM((2, PAD_C + HALO_C + TILE_ROWS, LANES), F32),
        pltpu.VMEM((N_KEPT, TILE_ROWS, GROUP), F32),
    ]
    assert len(slot) == N_SLOT_BUFFERS
    work = lambda *lead: pltpu.VMEM((*lead, PAD_C + HALO_C + TILE_ROWS, LANES), F32)
    return pl.pallas_call(
        functools.partial(_layer_kernel, tiles_per_seq=seq_len // TILE_ROWS, n_tiles=n_tiles),
        grid=(n_tiles + 1,),
        in_specs=in_specs,
        out_specs=pl.BlockSpec(tile, lambda i: (jnp.maximum(i - 1, 0), 0)),
        out_shape=jax.ShapeDtypeStruct((n, D_MODEL), F32),
        scratch_shapes=slot + slot + [
            work(2),
            work(2),
            work(),
            pltpu.VMEM((TILE_ROWS, GROUP), F32),
            pltpu.VMEM((TILE_ROWS, D_MODEL), BF16),
            pltpu.VMEM((TILE_ROWS, D_MODEL), BF16),
            pltpu.VMEM((SGU_BLOCK, 4 * SGU_BLOCK), BF16),
            pltpu.VMEM((D_MODEL, 12 * GROUP), BF16),
            pltpu.VMEM((D_MODEL, D_MODEL), BF16),
        ],
        compiler_params=pltpu.CompilerParams(
            dimension_semantics=("arbitrary",), vmem_limit_bytes=VMEM_LIMIT_BYTES),
        name="hybrid_layer",
    )(x2, x2, *stacked, *shared)


def _pack_small(b_in, conv_a_b, norm_a_g, norm_a_b, pool_scale, sgu_ln_g, sgu_ln_b, b_out, ln_g, ln_b,
                conv_b_w):
    depth, width = b_in.shape
    lanes = lambda v: jnp.pad(v, ((0, 0), (0, 0), (0, width - v.shape[-1])))
    row = lambda *vs: lanes(jnp.concatenate(vs, axis=-1)[:, None, :])
    blank = lambda n: jnp.zeros((depth, n, width), F32)
    return jnp.concatenate([
        row(b_in),
        row(conv_a_b, norm_a_g, norm_a_b, pool_scale, sgu_ln_g, sgu_ln_b),
        row(b_out, ln_g, ln_b),
        blank(CONV_B_ROW - 3),
        lanes(conv_b_w),
        blank(SMALL_ROWS - CONV_B_ROW - CONV_B_K),
    ], axis=1)


def _constants():
    head_id = np.arange(GROUP) // HEAD
    phead = (head_id[:, None] == head_id[None, :]).astype(np.float32) / HEAD
    win = np.asarray(POOL_WINDOWS, np.float32)[head_id]
    pos1 = np.arange(1, HALO_C + 1, dtype=np.float32)[:, None]
    corr = win[None, :] / np.minimum(pos1, win[None, :])
    return jnp.asarray(phead, BF16), jnp.asarray(corr, F32)


def kernel(x, ln_g, ln_b, w_in, b_in, conv_a_w, conv_a_b, norm_a_g, norm_a_b, conv_b_w, pool_w, pool_scale, sgu_ln_g, sgu_ln_b, sgu_w, sgu_bias, w_out, b_out):
    bsz, seq_len, d = x.shape
    depth = w_in.shape[0]
    phead, corr = _constants()
    n_heads = len(POOL_WINDOWS)
    same_head = jnp.eye(n_heads, dtype=F32)[None, :, None, :, None]
    pool_bd = (pool_w[:, :, :, None, :] * same_head).reshape(depth, GROUP, GROUP)
    sgu_wide = jnp.transpose(sgu_w, (0, 2, 1, 3)).reshape(depth, SGU_BLOCK, 4 * SGU_BLOCK)
    sgu_bias_tbl = jnp.repeat(jnp.transpose(sgu_bias, (0, 2, 1)), HEAD, axis=2)
    small = _pack_small(b_in, conv_a_b, norm_a_g, norm_a_b, pool_scale, sgu_ln_g, sgu_ln_b, b_out, ln_g, ln_b,
                        conv_b_w)
    stacked = (w_in, small, conv_a_w, pool_bd.astype(BF16), sgu_wide, sgu_bias_tbl, w_out)
    shared = (corr, phead)
    h = x.reshape(bsz * seq_len, d)
    for layer in range(depth):
        h = _layer(h, seq_len, layer, stacked, shared)
    return h.reshape(bsz, seq_len, d)
```

```python
import functools

import jax
import jax.numpy as jnp
import numpy as np
from jax.experimental import pallas as pl
from jax.experimental.pallas import tpu as pltpu

D_MODEL = 1024
DEPTH = 4
GROUP = 256
HEAD = 64
CONV_A_K = 31
CONV_B_K = 3
POOL_WINDOWS = (2, 4, 8, 16)
SGU_BLOCK = 128
CHUNK = 64
LN_EPS = 1e-5
ALPHA = float((2.0 * DEPTH) ** 0.25)

LANES = 128
SUBLANES = 8
TILE_ROWS = 512
NORM_ROWS = SGU_BLOCK
OUT_ROWS = 256
HALO_A = 32
HALO_B = 32
HALO_C = 16
PAD_C = 16
N_KEPT = 5
WEIGHT_CAST_ROWS = 128
CONV_A_UNROLL = 4
VMEM_LIMIT_BYTES = 56 * 1024 * 1024
N_LAYER_PARAMS = 7
SMALL_ROWS = 16
CONV_B_ROW = 8
N_SLOT_BUFFERS = 4

F32 = jnp.float32
BF16 = jnp.bfloat16


def _silu(v):
    return v * jax.nn.sigmoid(v)


def _dot(a, b):
    return jnp.dot(a, b, preferred_element_type=F32)


def _after(value, anchor, slow=False):
    bits = jax.lax.bitcast_convert_type(anchor, jnp.uint32)
    zero = jax.lax.shift_right_logical(jax.lax.shift_right_logical(bits, jnp.uint32(16)), jnp.uint32(16))
    if slow:
        total = jnp.sum(jax.lax.bitcast_convert_type(zero, F32), axis=-1, keepdims=True)
        zero = jax.lax.bitcast_convert_type(jnp.broadcast_to(total, value.shape), jnp.uint32)
    return jax.lax.bitcast_convert_type(jax.lax.bitcast_convert_type(value, jnp.uint32) | zero, F32)


def _layer_norm_rows(v, g, b):
    mu = jnp.mean(v, axis=-1, keepdims=True)
    d = v - mu
    var = jnp.mean(d * d, axis=-1, keepdims=True)
    return d * jax.lax.rsqrt(var + LN_EPS) * g + b


def _layer_kernel(*refs, tiles_per_seq, n_tiles):
    x_ref, xprev_ref = refs[0:2]
    params = refs[2:2 + N_LAYER_PARAMS]
    corr_ref, phead_ref, o_ref = refs[2 + N_LAYER_PARAMS:5 + N_LAYER_PARAMS]
    scratch = refs[5 + N_LAYER_PARAMS:]
    slots = (scratch[0:N_SLOT_BUFFERS], scratch[N_SLOT_BUFFERS:2 * N_SLOT_BUFFERS])
    s2buf, s4buf, s8buf, tbuf, mix, xb_ref, ws_ref, win_bf, wout_bf = scratch[2 * N_SLOT_BUFFERS:]
    win_ref, swb_ref, wout_ref = params[0], params[4], params[6]
    pid = pl.program_id(0)

    step = functools.partial(
        _pipeline_step, pid, tiles_per_seq, x_ref, xprev_ref, params, corr_ref, phead_ref, o_ref,
        s2buf, s4buf, s8buf, tbuf, mix, xb_ref, ws_ref, win_bf, wout_bf)

    @pl.when(pid == 0)
    def _first():
        def cast_rows(k, carry):
            rs = pl.ds(pl.multiple_of(k * WEIGHT_CAST_ROWS, WEIGHT_CAST_ROWS), WEIGHT_CAST_ROWS)
            win_bf[rs, :] = win_ref[rs, :].astype(BF16)
            wout_bf[rs, :] = wout_ref[rs, :].astype(BF16)
            return carry
        jax.lax.fori_loop(0, D_MODEL // WEIGHT_CAST_ROWS, cast_rows, 0)
        for slot in slots:
            slot[2][:, 0:PAD_C, :] = jnp.zeros((2, PAD_C, LANES), F32)
        s2buf[:, 0:PAD_C, :] = jnp.zeros((2, PAD_C, LANES), F32)
        wi = jax.lax.broadcasted_iota(jnp.int32, (SGU_BLOCK, 4 * SGU_BLOCK), 0)
        wj = jax.lax.broadcasted_iota(jnp.int32, (SGU_BLOCK, 4 * SGU_BLOCK), 1) % SGU_BLOCK
        ws_ref[...] = jnp.where(wj // CHUNK <= wi // CHUNK, swb_ref[...], 0.0).astype(BF16)
        step(slots[0], None)

    for parity in range(2):
        @pl.when((pid > 0) & (pid < n_tiles) & (pid % 2 == parity))
        def _middle(parity=parity):
            step(slots[parity], slots[1 - parity])

    @pl.when(pid == n_tiles)
    def _last():
        step(None, slots[(n_tiles - 1) % 2])


def _pipeline_step(pid, tiles_per_seq, x_ref, xprev_ref, params, corr_ref, phead_ref, o_ref,
                   s2buf, s4buf, s8buf, tbuf, mix, xb_ref, ws_ref, win_ref, wout_ref,
                   cur_slot, prev_slot):
    _, small_ref, caw_ref, pbd_ref, _, sbias_ref, _ = params
    bin_ref = small_ref.at[0:1, :]
    cab_ref, nag_ref, nab_ref, psc_ref, sg_ref, sb_ref = (
        small_ref.at[1:2, GROUP * i:GROUP * (i + 1)] for i in range(6))
    bout_ref, lng_ref, lnb_ref = (small_ref.at[2:3, D_MODEL * i:D_MODEL * (i + 1)] for i in range(3))
    cbw_ref = small_ref.at[CONV_B_ROW:CONV_B_ROW + CONV_B_K, 0:GROUP]
    rows = TILE_ROWS
    dz = jnp.minimum(pid, 0)
    c0 = PAD_C + HALO_C

    def shifted(ref, start, n):
        return ref[pl.ds(dz + start, n), :]

    if cur_slot is not None:
        abuf_w, bbuf_w, cbuf_w, kept_w = cur_slot
        xb_ref[...] = x_ref[...].astype(BF16)

    def proj(i):
        lo, hi = GROUP * i, GROUP * (i + 1)
        return _dot(xb_ref[...], win_ref[:, lo:hi]) + bin_ref[:, lo:hi]

    def put_slabs(buf, halo, val):
        for s in range(2):
            buf[s, halo:halo + rows, :] = val[:, LANES * s:LANES * (s + 1)]

    def proj_a():
        put_slabs(abuf_w, HALO_A, proj(0) * jax.nn.sigmoid(proj(1)))

    def proj_a_gate():
        kept_w[0] = _silu(proj(2))

    def proj_b_gate():
        kept_w[1] = proj(3) * _silu(proj(6))

    def proj_b():
        put_slabs(bbuf_w, HALO_B, proj(4) * proj(5))

    def proj_c():
        put_slabs(cbuf_w, c0, proj(7))

    def proj_c_gate():
        kept_w[2] = _silu(proj(8)) * psc_ref[...]

    def proj_d_gate():
        kept_w[3] = proj(9) * _silu(proj(11))

    def proj_d():
        kept_w[4] = proj(10)

    if prev_slot is not None:
        abuf_r, bbuf_r, cbuf_r, kept_r = prev_slot
        first_prev = ((pid - 1) % tiles_per_seq) == 0
        csrc = [cbuf_r.at[s] for s in range(2)]

    def pool_sums():
        n2 = rows + HALO_C
        for s in range(2):
            s2buf[s, PAD_C:PAD_C + n2, :] = shifted(csrc[s], PAD_C, n2) + shifted(csrc[s], PAD_C - 1, n2)
        for s in range(2):
            s4buf[s, PAD_C:PAD_C + n2, :] = (
                shifted(s2buf.at[s], PAD_C, n2) + shifted(s2buf.at[s], PAD_C - 2, n2))
        n8 = rows + 8
        s8buf[c0 - 8:c0 - 8 + n8, :] = shifted(s4buf.at[1], c0 - 8, n8) + shifted(s4buf.at[1], c0 - 12, n8)

    def half(h):
        return slice(OUT_ROWS * h, OUT_ROWS * (h + 1))

    conv_a_tail = []

    def conv_a(h):
        for s in range(2):
            lo, hi = LANES * s, LANES * (s + 1)
            src = abuf_r.at[s]
            bias = jnp.broadcast_to(cab_ref[:, lo:hi], (SUBLANES, LANES))
            for g in range(OUT_ROWS // (SUBLANES * CONV_A_UNROLL)):
                g0 = OUT_ROWS * h + SUBLANES * CONV_A_UNROLL * g
                base = HALO_A + g0 - (CONV_A_K - 1)
                start = _after(bias, conv_a_tail[-1], slow=len(conv_a_tail) % 2 == 0) if conv_a_tail else bias
                accs = [start] * CONV_A_UNROLL
                for k in range(CONV_A_K):
                    wk = jnp.broadcast_to(caw_ref[k:k + 1, lo:hi], (SUBLANES, LANES))
                    for j in range(CONV_A_UNROLL):
                        accs[j] = accs[j] + shifted(src, base + SUBLANES * j + k, SUBLANES) * wk
                for j in range(CONV_A_UNROLL):
                    tbuf[g0 + SUBLANES * j:g0 + SUBLANES * (j + 1), lo:hi] = accs[j]
                conv_a_tail.append(accs[-1])

    def prep_cd(h):
        n, r0 = OUT_ROWS, OUT_ROWS * h
        low_half = jax.lax.broadcasted_iota(jnp.int32, (1, LANES), 1) < HEAD
        cr = c0 + r0
        s16 = shifted(s8buf, cr, n) + shifted(s8buf, cr - 8, n)
        inv2, inv4, inv8, inv16 = (1.0 / w for w in POOL_WINDOWS)
        mean0 = jnp.where(low_half, s2buf[0, cr:cr + n, :] * inv2, s4buf[0, cr:cr + n, :] * inv4)
        mean1 = jnp.where(low_half, s8buf[cr:cr + n, :] * inv8, s16 * inv16)
        mean = jnp.concatenate([mean0, mean1], axis=-1)
        if h == 0:
            head_rows = mean[0:HALO_C, :] * jnp.where(first_prev, corr_ref[...], 1.0)
            mean = jnp.concatenate([head_rows, mean[HALO_C:, :]], axis=0)
        ch = jnp.concatenate([shifted(csrc[s], cr, n) for s in range(2)], axis=-1)
        pooled = (mean - ch).astype(BF16)
        v = _layer_norm_rows(kept_r[4, half(h), :], sg_ref[...], sb_ref[...]).astype(BF16)
        return pooled, v

    def mean_a(h):
        return _dot(tbuf[half(h), :].astype(BF16), phead_ref[...])

    def var_a(h, mu):
        d = tbuf[half(h), :] - mu
        tbuf[half(h), :] = d
        return _dot((d * d).astype(BF16), phead_ref[...])

    def dots_cd(pooled, v):
        yc = _dot(pooled, pbd_ref[...])
        head_of_lane = jax.lax.broadcasted_iota(jnp.int32, (1, GROUP), 1) // HEAD
        sp = []
        for q in range(OUT_ROWS // SGU_BLOCK):
            blk = v[SGU_BLOCK * q:SGU_BLOCK * (q + 1), :]
            vbig = jnp.concatenate(
                [jnp.where(head_of_lane == hd, blk, jnp.zeros_like(blk)) for hd in range(4)], axis=0)
            sp.append(_dot(ws_ref[...], vbig) + sbias_ref[...])
        return yc, jnp.concatenate(sp, axis=0)

    def finish(h, var, yc, sp):
        rs = half(h)
        r0, n = OUT_ROWS * h, OUT_ROWS
        an = tbuf[rs, :] * jax.lax.rsqrt(var + LN_EPS) * nag_ref[...] + nab_ref[...]
        mix[rs, 0:GROUP] = (_silu(an) * kept_r[0, rs, :]).astype(BF16)
        conv_b = []
        for s in range(2):
            lo, hi = LANES * s, LANES * (s + 1)
            src = bbuf_r.at[s]
            acc = shifted(src, HALO_B + r0, n) * cbw_ref[CONV_B_K - 1:CONV_B_K, lo:hi]
            for k in range(CONV_B_K - 1):
                acc = acc + shifted(src, HALO_B + r0 - (CONV_B_K - 1) + k, n) * cbw_ref[k:k + 1, lo:hi]
            conv_b.append(acc)
        mix[rs, GROUP:2 * GROUP] = (jnp.concatenate(conv_b, axis=-1) * kept_r[1, rs, :]).astype(BF16)
        mix[rs, 2 * GROUP:3 * GROUP] = (yc * kept_r[2, rs, :]).astype(BF16)
        mix[rs, 3 * GROUP:4 * GROUP] = (sp * kept_r[3, rs, :]).astype(BF16)

    def out_proj(h):
        o_ref[half(h), :] = _dot(mix[half(h), :], wout_ref[...]) + bout_ref[...]

    def post_norm(h):
        for q in range(OUT_ROWS * h, OUT_ROWS * (h + 1), NORM_ROWS):
            rs = slice(q, q + NORM_ROWS)
            r = ALPHA * xprev_ref[rs, :] + o_ref[rs, :]
            o_ref[rs, :] = _layer_norm_rows(r, lng_ref[...], lnb_ref[...])

    do1, do2 = cur_slot is not None, prev_slot is not None
    state = {}
    schedule = (
        (2, pool_sums), (2, lambda: conv_a(0)), (2, lambda: conv_a(1)),
        (1, proj_a), (1, proj_a_gate), (1, proj_b_gate), (1, proj_b), (1, proj_c),
        (2, lambda: state.update(cd0=prep_cd(0), cd1=prep_cd(1))),
        (2, lambda: state.update(mu0=mean_a(0), mu1=mean_a(1))),
        (1, proj_c_gate),
        (2, lambda: state.update(var0=var_a(0, state["mu0"]), var1=var_a(1, state["mu1"]))),
        (2, lambda: state.update(y0=dots_cd(*state["cd0"]), y1=dots_cd(*state["cd1"]))),
        (1, proj_d),
        (2, lambda: finish(0, state["var0"], *state["y0"])), (2, lambda: out_proj(0)),
        (2, lambda: finish(1, state["var1"], *state["y1"])), (2, lambda: out_proj(1)),
        (2, lambda: post_norm(0)),
        (1, proj_d_gate),
        (2, lambda: post_norm(1)),
    )
    for stage, piece in schedule:
        if (stage == 1 and do1) or (stage == 2 and do2):
            piece()

    if do1:
        if do2:
            fresh = (pid % tiles_per_seq) == 0
            abuf_w[:, 0:HALO_A, :] = jnp.where(fresh, 0.0, abuf_r[:, rows:rows + HALO_A, :])
            bbuf_w[:, 0:HALO_B, :] = jnp.where(fresh, 0.0, bbuf_r[:, rows:rows + HALO_B, :])
            cbuf_w[:, PAD_C:c0, :] = jnp.where(fresh, 0.0, cbuf_r[:, rows + PAD_C:rows + c0, :])
        else:
            abuf_w[:, 0:HALO_A, :] = jnp.zeros((2, HALO_A, LANES), F32)
            bbuf_w[:, 0:HALO_B, :] = jnp.zeros((2, HALO_B, LANES), F32)
            cbuf_w[:, PAD_C:c0, :] = jnp.zeros((2, HALO_C, LANES), F32)


def _layer_param_spec(layer, shape):
    return pl.BlockSpec((None,) + tuple(shape[1:]), lambda i: (layer,) + (0,) * (len(shape) - 1))


def _shared_spec(shape):
    return pl.BlockSpec(shape, lambda i: (0,) * len(shape))


def _layer(x2, seq_len, layer, stacked, shared):
    n = x2.shape[0]
    assert seq_len % TILE_ROWS == 0 and TILE_ROWS == 2 * OUT_ROWS and OUT_ROWS % SGU_BLOCK == 0
    assert len(stacked) == N_LAYER_PARAMS
    n_tiles = n // TILE_ROWS
    tile = (TILE_ROWS, D_MODEL)
    in_specs = [pl.BlockSpec(tile, lambda i: (jnp.minimum(i, n_tiles - 1), 0)),
                pl.BlockSpec(tile, lambda i: (jnp.maximum(i - 1, 0), 0))]
    in_specs += [_layer_param_spec(layer, p.shape) for p in stacked]
    in_specs += [_shared_spec(p.shape) for p in shared]
    slot = [
        pltpu.VMEM((2, HALO_A + TILE_ROWS, LANES), F32),
        pltpu.VMEM((2, HALO_B + TILE_ROWS, LANES), F32),
        pltpu.VMEM((2, PAD_C + HALO_C + TILE_ROWS, LANES), F32),
        pltpu.VMEM((N_KEPT, TILE_ROWS, GROUP), F32),
    ]
    assert len(slot) == N_SLOT_BUFFERS
    work = lambda *lead: pltpu.VMEM((*lead, PAD_C + HALO_C + TILE_ROWS, LANES), F32)
    return pl.pallas_call(
        functools.partial(_layer_kernel, tiles_per_seq=seq_len // TILE_ROWS, n_tiles=n_tiles),
        grid=(n_tiles + 1,),
        in_specs=in_specs,
        out_specs=pl.BlockSpec(tile, lambda i: (jnp.maximum(i - 1, 0), 0)),
        out_shape=jax.ShapeDtypeStruct((n, D_MODEL), F32),
        scratch_shapes=slot + slot + [
            work(2),
            work(2),
            work(),
            pltpu.VMEM((TILE_ROWS, GROUP), F32),
            pltpu.VMEM((TILE_ROWS, D_MODEL), BF16),
            pltpu.VMEM((TILE_ROWS, D_MODEL), BF16),
            pltpu.VMEM((SGU_BLOCK, 4 * SGU_BLOCK), BF16),
            pltpu.VMEM((D_MODEL, 12 * GROUP), BF16),
            pltpu.VMEM((D_MODEL, D_MODEL), BF16),
        ],
        compiler_params=pltpu.CompilerParams(
            dimension_semantics=("arbitrary",), vmem_limit_bytes=VMEM_LIMIT_BYTES),
        name="hybrid_layer",
    )(x2, x2, *stacked, *shared)


def _pack_small(b_in, conv_a_b, norm_a_g, norm_a_b, pool_scale, sgu_ln_g, sgu_ln_b, b_out, ln_g, ln_b,
                conv_b_w):
    depth, width = b_in.shape
    lanes = lambda v: jnp.pad(v, ((0, 0), (0, 0), (0, width - v.shape[-1])))
    row = lambda *vs: lanes(jnp.concatenate(vs, axis=-1)[:, None, :])
    blank = lambda n: jnp.zeros((depth, n, width), F32)
    return jnp.concatenate([
        row(b_in),
        row(conv_a_b, norm_a_g, norm_a_b, pool_scale, sgu_ln_g, sgu_ln_b),
        row(b_out, ln_g, ln_b),
        blank(CONV_B_ROW - 3),
        lanes(conv_b_w),
        blank(SMALL_ROWS - CONV_B_ROW - CONV_B_K),
    ], axis=1)


def _constants():
    head_id = np.arange(GROUP) // HEAD
    phead = (head_id[:, None] == head_id[None, :]).astype(np.float32) / HEAD
    win = np.asarray(POOL_WINDOWS, np.float32)[head_id]
    pos1 = np.arange(1, HALO_C + 1, dtype=np.float32)[:, None]
    corr = win[None, :] / np.minimum(pos1, win[None, :])
    return jnp.asarray(phead, BF16), jnp.asarray(corr, F32)


def kernel(x, ln_g, ln_b, w_in, b_in, conv_a_w, conv_a_b, norm_a_g, norm_a_b, conv_b_w, pool_w, pool_scale, sgu_ln_g, sgu_ln_b, sgu_w, sgu_bias, w_out, b_out):
    bsz, seq_len, d = x.shape
    depth = w_in.shape[0]
    phead, corr = _constants()
    n_heads = len(POOL_WINDOWS)
    same_head = jnp.eye(n_heads, dtype=F32)[None, :, None, :, None]
    pool_bd = (pool_w[:, :, :, None, :] * same_head).reshape(depth, GROUP, GROUP)
    sgu_wide = jnp.transpose(sgu_w, (0, 2, 1, 3)).reshape(depth, SGU_BLOCK, 4 * SGU_BLOCK)
    sgu_bias_tbl = jnp.repeat(jnp.transpose(sgu_bias, (0, 2, 1)), HEAD, axis=2)
    small = _pack_small(b_in, conv_a_b, norm_a_g, norm_a_b, pool_scale, sgu_ln_g, sgu_ln_b, b_out, ln_g, ln_b,
                        conv_b_w)
    stacked = (w_in, small, conv_a_w, pool_bd.astype(BF16), sgu_wide, sgu_bias_tbl, w_out)
    shared = (corr, phead)
    h = x.reshape(bsz * seq_len, d)
    for layer in range(depth):
        h = _layer(h, seq_len, layer, stacked, shared)
    return h.reshape(bsz, seq_len, d)
```

```python
import functools

import jax
import jax.numpy as jnp
import numpy as np
from jax.experimental import pallas as pl
from jax.experimental.pallas import tpu as pltpu

D_MODEL = 1024
DEPTH = 4
GROUP = 256
HEAD = 64
CONV_A_K = 31
CONV_B_K = 3
POOL_WINDOWS = (2, 4, 8, 16)
SGU_BLOCK = 128
CHUNK = 64
LN_EPS = 1e-5
ALPHA = float((2.0 * DEPTH) ** 0.25)

LANES = 128
SUBLANES = 8
TILE_ROWS = 512
NORM_ROWS = SGU_BLOCK
OUT_ROWS = 256
HALO_A = 32
HALO_B = 32
HALO_C = 16
PAD_C = 16
N_KEPT = 5
WEIGHT_CAST_ROWS = 128
CONV_A_UNROLL = 4
VMEM_LIMIT_BYTES = 56 * 1024 * 1024
N_LAYER_PARAMS = 7
SMALL_ROWS = 16
CONV_B_ROW = 8
N_SLOT_BUFFERS = 4

F32 = jnp.float32
BF16 = jnp.bfloat16


def _sigmoid(v):
    return 0.5 * jnp.tanh(0.5 * v) + 0.5


def _silu(v):
    half_v = 0.5 * v
    return half_v * jnp.tanh(half_v) + half_v


def _dot(a, b):
    return jnp.dot(a, b, preferred_element_type=F32)


def _after(value, anchor, slow=False):
    bits = jax.lax.bitcast_convert_type(anchor, jnp.uint32)
    zero = jax.lax.shift_right_logical(jax.lax.shift_right_logical(bits, jnp.uint32(16)), jnp.uint32(16))
    if slow:
        total = jnp.sum(jax.lax.bitcast_convert_type(zero, F32), axis=-1, keepdims=True)
        zero = jax.lax.bitcast_convert_type(jnp.broadcast_to(total, value.shape), jnp.uint32)
    return jax.lax.bitcast_convert_type(jax.lax.bitcast_convert_type(value, jnp.uint32) | zero, F32)


def _layer_norm_rows(v, g, b):
    mu = jnp.mean(v, axis=-1, keepdims=True)
    d = v - mu
    var = jnp.mean(d * d, axis=-1, keepdims=True)
    return d * jax.lax.rsqrt(var + LN_EPS) * g + b


def _layer_kernel(*refs, tiles_per_seq, n_tiles):
    x_ref, xprev_ref = refs[0:2]
    params = refs[2:2 + N_LAYER_PARAMS]
    corr_ref, phead_ref, o_ref = refs[2 + N_LAYER_PARAMS:5 + N_LAYER_PARAMS]
    scratch = refs[5 + N_LAYER_PARAMS:]
    slots = (scratch[0:N_SLOT_BUFFERS], scratch[N_SLOT_BUFFERS:2 * N_SLOT_BUFFERS])
    s2buf, s4buf, s8buf, tbuf, mix, xb_ref, ws_ref, win_bf, wout_bf = scratch[2 * N_SLOT_BUFFERS:]
    win_ref, swb_ref, wout_ref = params[0], params[4], params[6]
    pid = pl.program_id(0)

    step = functools.partial(
        _pipeline_step, pid, tiles_per_seq, x_ref, xprev_ref, params, corr_ref, phead_ref, o_ref,
        s2buf, s4buf, s8buf, tbuf, mix, xb_ref, ws_ref, win_bf, wout_bf)

    @pl.when(pid == 0)
    def _first():
        def cast_rows(k, carry):
            rs = pl.ds(pl.multiple_of(k * WEIGHT_CAST_ROWS, WEIGHT_CAST_ROWS), WEIGHT_CAST_ROWS)
            win_bf[rs, :] = win_ref[rs, :].astype(BF16)
            wout_bf[rs, :] = wout_ref[rs, :].astype(BF16)
            return carry
        jax.lax.fori_loop(0, D_MODEL // WEIGHT_CAST_ROWS, cast_rows, 0)
        for slot in slots:
            slot[2][:, 0:PAD_C, :] = jnp.zeros((2, PAD_C, LANES), F32)
        s2buf[:, 0:PAD_C, :] = jnp.zeros((2, PAD_C, LANES), F32)
        wi = jax.lax.broadcasted_iota(jnp.int32, (SGU_BLOCK, 4 * SGU_BLOCK), 0)
        wj = jax.lax.broadcasted_iota(jnp.int32, (SGU_BLOCK, 4 * SGU_BLOCK), 1) % SGU_BLOCK
        ws_ref[...] = jnp.where(wj // CHUNK <= wi // CHUNK, swb_ref[...], 0.0).astype(BF16)
        step(slots[0], None)

    for parity in range(2):
        @pl.when((pid > 0) & (pid < n_tiles) & (pid % 2 == parity))
        def _middle(parity=parity):
            step(slots[parity], slots[1 - parity])

    @pl.when(pid == n_tiles)
    def _last():
        step(None, slots[(n_tiles - 1) % 2])


def _pipeline_step(pid, tiles_per_seq, x_ref, xprev_ref, params, corr_ref, phead_ref, o_ref,
                   s2buf, s4buf, s8buf, tbuf, mix, xb_ref, ws_ref, win_ref, wout_ref,
                   cur_slot, prev_slot):
    _, small_ref, caw_ref, pbd_ref, _, sbias_ref, _ = params
    bin_ref = small_ref.at[0:1, :]
    cab_ref, nag_ref, nab_ref, psc_ref, sg_ref, sb_ref = (
        small_ref.at[1:2, GROUP * i:GROUP * (i + 1)] for i in range(6))
    bout_ref, lng_ref, lnb_ref = (small_ref.at[2:3, D_MODEL * i:D_MODEL * (i + 1)] for i in range(3))
    cbw_ref = small_ref.at[CONV_B_ROW:CONV_B_ROW + CONV_B_K, 0:GROUP]
    rows = TILE_ROWS
    dz = jnp.minimum(pid, 0)
    c0 = PAD_C + HALO_C

    def shifted(ref, start, n):
        return ref[pl.ds(dz + start, n), :]

    if cur_slot is not None:
        abuf_w, bbuf_w, cbuf_w, kept_w = cur_slot
        xb_ref[...] = x_ref[...].astype(BF16)

    def proj(i):
        lo, hi = GROUP * i, GROUP * (i + 1)
        return _dot(xb_ref[...], win_ref[:, lo:hi]) + bin_ref[:, lo:hi]

    def put_slabs(buf, halo, val):
        for s in range(2):
            buf[s, halo:halo + rows, :] = val[:, LANES * s:LANES * (s + 1)]

    def proj_a():
        put_slabs(abuf_w, HALO_A, proj(0) * _sigmoid(proj(1)))

    def proj_a_gate():
        kept_w[0] = _silu(proj(2))

    def proj_b_gate():
        kept_w[1] = proj(3) * _silu(proj(6))

    def proj_b():
        put_slabs(bbuf_w, HALO_B, proj(4) * proj(5))

    def proj_c():
        put_slabs(cbuf_w, c0, proj(7))

    def proj_c_gate():
        kept_w[2] = _silu(proj(8)) * psc_ref[...]

    def proj_d_gate():
        kept_w[3] = proj(9) * _silu(proj(11))

    def proj_d():
        kept_w[4] = proj(10)

    if prev_slot is not None:
        abuf_r, bbuf_r, cbuf_r, kept_r = prev_slot
        first_prev = ((pid - 1) % tiles_per_seq) == 0
        csrc = [cbuf_r.at[s] for s in range(2)]

    def pool_sums():
        n2 = rows + HALO_C
        for s in range(2):
            s2buf[s, PAD_C:PAD_C + n2, :] = shifted(csrc[s], PAD_C, n2) + shifted(csrc[s], PAD_C - 1, n2)
        for s in range(2):
            s4buf[s, PAD_C:PAD_C + n2, :] = (
                shifted(s2buf.at[s], PAD_C, n2) + shifted(s2buf.at[s], PAD_C - 2, n2))
        n8 = rows + 8
        s8buf[c0 - 8:c0 - 8 + n8, :] = shifted(s4buf.at[1], c0 - 8, n8) + shifted(s4buf.at[1], c0 - 12, n8)

    def half(h):
        return slice(OUT_ROWS * h, OUT_ROWS * (h + 1))

    conv_a_tail = []

    def conv_a(h):
        for s in range(2):
            lo, hi = LANES * s, LANES * (s + 1)
            src = abuf_r.at[s]
            bias = jnp.broadcast_to(cab_ref[:, lo:hi], (SUBLANES, LANES))
            for g in range(OUT_ROWS // (SUBLANES * CONV_A_UNROLL)):
                g0 = OUT_ROWS * h + SUBLANES * CONV_A_UNROLL * g
                base = HALO_A + g0 - (CONV_A_K - 1)
                start = _after(bias, conv_a_tail[-1], slow=len(conv_a_tail) % 2 == 0) if conv_a_tail else bias
                accs = [start] * CONV_A_UNROLL
                for k in range(CONV_A_K):
                    wk = jnp.broadcast_to(caw_ref[k:k + 1, lo:hi], (SUBLANES, LANES))
                    for j in range(CONV_A_UNROLL):
                        accs[j] = accs[j] + shifted(src, base + SUBLANES * j + k, SUBLANES) * wk
                for j in range(CONV_A_UNROLL):
                    tbuf[g0 + SUBLANES * j:g0 + SUBLANES * (j + 1), lo:hi] = accs[j]
                conv_a_tail.append(accs[-1])

    def prep_cd(h):
        n, r0 = OUT_ROWS, OUT_ROWS * h
        low_half = jax.lax.broadcasted_iota(jnp.int32, (1, LANES), 1) < HEAD
        cr = c0 + r0
        s16 = shifted(s8buf, cr, n) + shifted(s8buf, cr - 8, n)
        inv2, inv4, inv8, inv16 = (1.0 / w for w in POOL_WINDOWS)
        mean0 = jnp.where(low_half, s2buf[0, cr:cr + n, :] * inv2, s4buf[0, cr:cr + n, :] * inv4)
        mean1 = jnp.where(low_half, s8buf[cr:cr + n, :] * inv8, s16 * inv16)
        mean = jnp.concatenate([mean0, mean1], axis=-1)
        if h == 0:
            head_rows = mean[0:HALO_C, :] * jnp.where(first_prev, corr_ref[...], 1.0)
            mean = jnp.concatenate([head_rows, mean[HALO_C:, :]], axis=0)
        ch = jnp.concatenate([shifted(csrc[s], cr, n) for s in range(2)], axis=-1)
        pooled = (mean - ch).astype(BF16)
        v = _layer_norm_rows(kept_r[4, half(h), :], sg_ref[...], sb_ref[...]).astype(BF16)
        return pooled, v

    def mean_a(h):
        return _dot(tbuf[half(h), :].astype(BF16), phead_ref[...])

    def var_a(h, mu):
        d = tbuf[half(h), :] - mu
        tbuf[half(h), :] = d
        return _dot((d * d).astype(BF16), phead_ref[...])

    def dots_cd(pooled, v):
        yc = _dot(pooled, pbd_ref[...])
        head_of_lane = jax.lax.broadcasted_iota(jnp.int32, (1, GROUP), 1) // HEAD
        sp = []
        for q in range(OUT_ROWS // SGU_BLOCK):
            blk = v[SGU_BLOCK * q:SGU_BLOCK * (q + 1), :]
            vbig = jnp.concatenate(
                [jnp.where(head_of_lane == hd, blk, jnp.zeros_like(blk)) for hd in range(4)], axis=0)
            sp.append(_dot(ws_ref[...], vbig) + sbias_ref[...])
        return yc, jnp.concatenate(sp, axis=0)

    def finish(h, var, yc, sp):
        rs = half(h)
        r0, n = OUT_ROWS * h, OUT_ROWS
        an = tbuf[rs, :] * jax.lax.rsqrt(var + LN_EPS) * nag_ref[...] + nab_ref[...]
        mix[rs, 0:GROUP] = (_silu(an) * kept_r[0, rs, :]).astype(BF16)
        conv_b = []
        for s in range(2):
            lo, hi = LANES * s, LANES * (s + 1)
            src = bbuf_r.at[s]
            acc = shifted(src, HALO_B + r0, n) * cbw_ref[CONV_B_K - 1:CONV_B_K, lo:hi]
            for k in range(CONV_B_K - 1):
                acc = acc + shifted(src, HALO_B + r0 - (CONV_B_K - 1) + k, n) * cbw_ref[k:k + 1, lo:hi]
            conv_b.append(acc)
        mix[rs, GROUP:2 * GROUP] = (jnp.concatenate(conv_b, axis=-1) * kept_r[1, rs, :]).astype(BF16)
        mix[rs, 2 * GROUP:3 * GROUP] = (yc * kept_r[2, rs, :]).astype(BF16)
        mix[rs, 3 * GROUP:4 * GROUP] = (sp * kept_r[3, rs, :]).astype(BF16)

    def out_proj(h):
        o_ref[half(h), :] = _dot(mix[half(h), :], wout_ref[...]) + bout_ref[...]

    def post_norm(h):
        for q in range(OUT_ROWS * h, OUT_ROWS * (h + 1), NORM_ROWS):
            rs = slice(q, q + NORM_ROWS)
            r = ALPHA * xprev_ref[rs, :] + o_ref[rs, :]
            o_ref[rs, :] = _layer_norm_rows(r, lng_ref[...], lnb_ref[...])

    do1, do2 = cur_slot is not None, prev_slot is not None
    state = {}
    schedule = (
        (2, pool_sums), (2, lambda: conv_a(0)), (2, lambda: conv_a(1)),
        (1, proj_a), (1, proj_a_gate), (1, proj_b_gate), (1, proj_b), (1, proj_c),
        (2, lambda: state.update(cd0=prep_cd(0), cd1=prep_cd(1))),
        (2, lambda: state.update(mu0=mean_a(0), mu1=mean_a(1))),
        (1, proj_c_gate),
        (2, lambda: state.update(var0=var_a(0, state["mu0"]), var1=var_a(1, state["mu1"]))),
        (2, lambda: state.update(y0=dots_cd(*state["cd0"]), y1=dots_cd(*state["cd1"]))),
        (1, proj_d),
        (2, lambda: finish(0, state["var0"], *state["y0"])), (2, lambda: out_proj(0)),
        (2, lambda: finish(1, state["var1"], *state["y1"])), (2, lambda: out_proj(1)),
        (2, lambda: post_norm(0)),
        (1, proj_d_gate),
        (2, lambda: post_norm(1)),
    )
    for stage, piece in schedule:
        if (stage == 1 and do1) or (stage == 2 and do2):
            piece()

    if do1:
        if do2:
            fresh = (pid % tiles_per_seq) == 0
            abuf_w[:, 0:HALO_A, :] = jnp.where(fresh, 0.0, abuf_r[:, rows:rows + HALO_A, :])
            bbuf_w[:, 0:HALO_B, :] = jnp.where(fresh, 0.0, bbuf_r[:, rows:rows + HALO_B, :])
            cbuf_w[:, PAD_C:c0, :] = jnp.where(fresh, 0.0, cbuf_r[:, rows + PAD_C:rows + c0, :])
        else:
            abuf_w[:, 0:HALO_A, :] = jnp.zeros((2, HALO_A, LANES), F32)
            bbuf_w[:, 0:HALO_B, :] = jnp.zeros((2, HALO_B, LANES), F32)
            cbuf_w[:, PAD_C:c0, :] = jnp.zeros((2, HALO_C, LANES), F32)


def _layer_param_spec(layer, shape):
    return pl.BlockSpec((None,) + tuple(shape[1:]), lambda i: (layer,) + (0,) * (len(shape) - 1))


def _shared_spec(shape):
    return pl.BlockSpec(shape, lambda i: (0,) * len(shape))


def _layer(x2, seq_len, layer, stacked, shared):
    n = x2.shape[0]
    assert seq_len % TILE_ROWS == 0 and TILE_ROWS == 2 * OUT_ROWS and OUT_ROWS % SGU_BLOCK == 0
    assert len(stacked) == N_LAYER_PARAMS
    n_tiles = n // TILE_ROWS
    tile = (TILE_ROWS, D_MODEL)
    in_specs = [pl.BlockSpec(tile, lambda i: (jnp.minimum(i, n_tiles - 1), 0)),
                pl.BlockSpec(tile, lambda i: (jnp.maximum(i - 1, 0), 0))]
    in_specs += [_layer_param_spec(layer, p.shape) for p in stacked]
    in_specs += [_shared_spec(p.shape) for p in shared]
    slot = [
        pltpu.VMEM((2, HALO_A + TILE_ROWS, LANES), F32),
        pltpu.VMEM((2, HALO_B + TILE_ROWS, LANES), F32),
        pltpu.VMEM((2, PAD_C + HALO_C + TILE_ROWS, LANES), F32),
        pltpu.VMEM((N_KEPT, TILE_ROWS, GROUP), F32),
    ]
    assert len(slot) == N_SLOT_BUFFERS
    work = lambda *lead: pltpu.VMEM((*lead, PAD_C + HALO_C + TILE_ROWS, LANES), F32)
    return pl.pallas_call(
        functools.partial(_layer_kernel, tiles_per_seq=seq_len // TILE_ROWS, n_tiles=n_tiles),
        grid=(n_tiles + 1,),
        in_specs=in_specs,
        out_specs=pl.BlockSpec(tile, lambda i: (jnp.maximum(i - 1, 0), 0)),
        out_shape=jax.ShapeDtypeStruct((n, D_MODEL), F32),
        scratch_shapes=slot + slot + [
            work(2),
            work(2),
            work(),
            pltpu.VMEM((TILE_ROWS, GROUP), F32),
            pltpu.VMEM((TILE_ROWS, D_MODEL), BF16),
            pltpu.VMEM((TILE_ROWS, D_MODEL), BF16),
            pltpu.VMEM((SGU_BLOCK, 4 * SGU_BLOCK), BF16),
            pltpu.VMEM((D_MODEL, 12 * GROUP), BF16),
            pltpu.VMEM((D_MODEL, D_MODEL), BF16),
        ],
        compiler_params=pltpu.CompilerParams(
            dimension_semantics=("arbitrary",), vmem_limit_bytes=VMEM_LIMIT_BYTES),
        name="hybrid_layer",
    )(x2, x2, *stacked, *shared)


def _pack_small(b_in, conv_a_b, norm_a_g, norm_a_b, pool_scale, sgu_ln_g, sgu_ln_b, b_out, ln_g, ln_b,
                conv_b_w):
    depth, width = b_in.shape
    lanes = lambda v: jnp.pad(v, ((0, 0), (0, 0), (0, width - v.shape[-1])))
    row = lambda *vs: lanes(jnp.concatenate(vs, axis=-1)[:, None, :])
    blank = lambda n: jnp.zeros((depth, n, width), F32)
    return jnp.concatenate([
        row(b_in),
        row(conv_a_b, norm_a_g, norm_a_b, pool_scale, sgu_ln_g, sgu_ln_b),
        row(b_out, ln_g, ln_b),
        blank(CONV_B_ROW - 3),
        lanes(conv_b_w),
        blank(SMALL_ROWS - CONV_B_ROW - CONV_B_K),
    ], axis=1)


def _constants():
    head_id = np.arange(GROUP) // HEAD
    phead = (head_id[:, None] == head_id[None, :]).astype(np.float32) / HEAD
    win = np.asarray(POOL_WINDOWS, np.float32)[head_id]
    pos1 = np.arange(1, HALO_C + 1, dtype=np.float32)[:, None]
    corr = win[None, :] / np.minimum(pos1, win[None, :])
    return jnp.asarray(phead, BF16), jnp.asarray(corr, F32)


def kernel(x, ln_g, ln_b, w_in, b_in, conv_a_w, conv_a_b, norm_a_g, norm_a_b, conv_b_w, pool_w, pool_scale, sgu_ln_g, sgu_ln_b, sgu_w, sgu_bias, w_out, b_out):
    bsz, seq_len, d = x.shape
    depth = w_in.shape[0]
    phead, corr = _constants()
    n_heads = len(POOL_WINDOWS)
    same_head = jnp.eye(n_heads, dtype=F32)[None, :, None, :, None]
    pool_bd = (pool_w[:, :, :, None, :] * same_head).reshape(depth, GROUP, GROUP)
    sgu_wide = jnp.transpose(sgu_w, (0, 2, 1, 3)).reshape(depth, SGU_BLOCK, 4 * SGU_BLOCK)
    sgu_bias_tbl = jnp.repeat(jnp.transpose(sgu_bias, (0, 2, 1)), HEAD, axis=2)
    small = _pack_small(b_in, conv_a_b, norm_a_g, norm_a_b, pool_scale, sgu_ln_g, sgu_ln_b, b_out, ln_g, ln_b,
                        conv_b_w)
    stacked = (w_in, small, conv_a_w, pool_bd.astype(BF16), sgu_wide, sgu_bias_tbl, w_out)
    shared = (corr, phead)
    h = x.reshape(bsz * seq_len, d)
    for layer in range(depth):
        h = _layer(h, seq_len, layer, stacked, shared)
    return h.reshape(bsz, seq_len, d)
```

```python
import functools

import jax
import jax.numpy as jnp
import numpy as np
from jax.experimental import pallas as pl
from jax.experimental.pallas import tpu as pltpu

D_MODEL = 1024
DEPTH = 4
GROUP = 256
HEAD = 64
CONV_A_K = 31
CONV_B_K = 3
POOL_WINDOWS = (2, 4, 8, 16)
SGU_BLOCK = 128
CHUNK = 64
LN_EPS = 1e-5
ALPHA = float((2.0 * DEPTH) ** 0.25)

LANES = 128
SUBLANES = 8
TILE_ROWS = 512
NORM_ROWS = SGU_BLOCK
OUT_ROWS = 256
HALO_A = 32
HALO_B = 32
HALO_C = 16
PAD_C = 16
N_KEPT = 5
GATE_SLICES = (1, 2, 6, 8, 11)
HALVED_SLICES = (0,) + GATE_SLICES
WEIGHT_CAST_ROWS = 128
CONV_A_UNROLL = 4
VMEM_LIMIT_BYTES = 56 * 1024 * 1024
N_LAYER_PARAMS = 7
SMALL_ROWS = 16
CONV_B_ROW = 8
N_SLOT_BUFFERS = 4

F32 = jnp.float32
BF16 = jnp.bfloat16


def _silu_of_twice(half_v):
    return half_v * jnp.tanh(half_v) + half_v


def _dot(a, b):
    return jnp.dot(a, b, preferred_element_type=F32)


def _after(value, anchor, slow=False):
    bits = jax.lax.bitcast_convert_type(anchor, jnp.uint32)
    zero = jax.lax.shift_right_logical(jax.lax.shift_right_logical(bits, jnp.uint32(16)), jnp.uint32(16))
    if slow:
        total = jnp.sum(jax.lax.bitcast_convert_type(zero, F32), axis=-1, keepdims=True)
        zero = jax.lax.bitcast_convert_type(jnp.broadcast_to(total, value.shape), jnp.uint32)
    return jax.lax.bitcast_convert_type(jax.lax.bitcast_convert_type(value, jnp.uint32) | zero, F32)


def _layer_norm_rows(v, g, b):
    mu = jnp.mean(v, axis=-1, keepdims=True)
    d = v - mu
    var = jnp.mean(d * d, axis=-1, keepdims=True)
    return d * jax.lax.rsqrt(var + LN_EPS) * g + b


def _layer_kernel(*refs, tiles_per_seq, n_tiles):
    x_ref, xprev_ref = refs[0:2]
    params = refs[2:2 + N_LAYER_PARAMS]
    corr_ref, phead_ref, o_ref = refs[2 + N_LAYER_PARAMS:5 + N_LAYER_PARAMS]
    scratch = refs[5 + N_LAYER_PARAMS:]
    slots = (scratch[0:N_SLOT_BUFFERS], scratch[N_SLOT_BUFFERS:2 * N_SLOT_BUFFERS])
    s2buf, s4buf, s8buf, tbuf, mix, xb_ref, ws_ref, win_bf, wout_bf = scratch[2 * N_SLOT_BUFFERS:]
    win_ref, swb_ref, wout_ref = params[0], params[4], params[6]
    pid = pl.program_id(0)

    step = functools.partial(
        _pipeline_step, pid, tiles_per_seq, x_ref, xprev_ref, params, corr_ref, phead_ref, o_ref,
        s2buf, s4buf, s8buf, tbuf, mix, xb_ref, ws_ref, win_bf, wout_bf)

    @pl.when(pid == 0)
    def _first():
        slice_of_lane = jax.lax.broadcasted_iota(jnp.int32, (1, 12 * GROUP), 1) // GROUP
        is_halved = functools.reduce(jnp.logical_or, [slice_of_lane == g for g in HALVED_SLICES])
        gate_scale = jnp.where(is_halved, 0.5, 1.0)

        def cast_rows(k, carry):
            rs = pl.ds(pl.multiple_of(k * WEIGHT_CAST_ROWS, WEIGHT_CAST_ROWS), WEIGHT_CAST_ROWS)
            win_bf[rs, :] = (win_ref[rs, :] * gate_scale).astype(BF16)
            wout_bf[rs, :] = wout_ref[rs, :].astype(BF16)
            return carry
        jax.lax.fori_loop(0, D_MODEL // WEIGHT_CAST_ROWS, cast_rows, 0)
        for slot in slots:
            slot[2][:, 0:PAD_C, :] = jnp.zeros((2, PAD_C, LANES), F32)
        s2buf[:, 0:PAD_C, :] = jnp.zeros((2, PAD_C, LANES), F32)
        wi = jax.lax.broadcasted_iota(jnp.int32, (SGU_BLOCK, 4 * SGU_BLOCK), 0)
        wj = jax.lax.broadcasted_iota(jnp.int32, (SGU_BLOCK, 4 * SGU_BLOCK), 1) % SGU_BLOCK
        ws_ref[...] = jnp.where(wj // CHUNK <= wi // CHUNK, swb_ref[...], 0.0).astype(BF16)
        step(slots[0], None)

    for parity in range(2):
        @pl.when((pid > 0) & (pid < n_tiles) & (pid % 2 == parity))
        def _middle(parity=parity):
            step(slots[parity], slots[1 - parity])

    @pl.when(pid == n_tiles)
    def _last():
        step(None, slots[(n_tiles - 1) % 2])


def _pipeline_step(pid, tiles_per_seq, x_ref, xprev_ref, params, corr_ref, phead_ref, o_ref,
                   s2buf, s4buf, s8buf, tbuf, mix, xb_ref, ws_ref, win_ref, wout_ref,
                   cur_slot, prev_slot):
    _, small_ref, caw_ref, pbd_ref, _, sbias_ref, _ = params
    bin_ref = small_ref.at[0:1, :]
    cab_ref, nag_ref, nab_ref, psc_ref, sg_ref, sb_ref = (
        small_ref.at[1:2, GROUP * i:GROUP * (i + 1)] for i in range(6))
    bout_ref, lng_ref, lnb_ref = (small_ref.at[2:3, D_MODEL * i:D_MODEL * (i + 1)] for i in range(3))
    cbw_ref = small_ref.at[CONV_B_ROW:CONV_B_ROW + CONV_B_K, 0:GROUP]
    rows = TILE_ROWS
    dz = jnp.minimum(pid, 0)
    c0 = PAD_C + HALO_C

    def shifted(ref, start, n):
        return ref[pl.ds(dz + start, n), :]

    if cur_slot is not None:
        abuf_w, bbuf_w, cbuf_w, kept_w = cur_slot
        xb_ref[...] = x_ref[...].astype(BF16)

    def proj(i):
        lo, hi = GROUP * i, GROUP * (i + 1)
        return _dot(xb_ref[...], win_ref[:, lo:hi]) + bin_ref[:, lo:hi]

    def proj_half(i):
        assert i in HALVED_SLICES
        lo, hi = GROUP * i, GROUP * (i + 1)
        return _dot(xb_ref[...], win_ref[:, lo:hi]) + 0.5 * bin_ref[:, lo:hi]

    def put_slabs(buf, halo, val):
        for s in range(2):
            buf[s, halo:halo + rows, :] = val[:, LANES * s:LANES * (s + 1)]

    def proj_a():
        half_val = proj_half(0)
        put_slabs(abuf_w, HALO_A, half_val * jnp.tanh(proj_half(1)) + half_val)

    def proj_a_gate():
        kept_w[0] = _silu_of_twice(proj_half(2))

    def proj_b_gate():
        kept_w[1] = proj(3) * _silu_of_twice(proj_half(6))

    def proj_b():
        put_slabs(bbuf_w, HALO_B, proj(4) * proj(5))

    def proj_c():
        put_slabs(cbuf_w, c0, proj(7))

    def proj_c_gate():
        kept_w[2] = _silu_of_twice(proj_half(8)) * psc_ref[...]

    def proj_d_gate():
        kept_w[3] = proj(9) * _silu_of_twice(proj_half(11))

    def proj_d():
        kept_w[4] = proj(10)

    if prev_slot is not None:
        abuf_r, bbuf_r, cbuf_r, kept_r = prev_slot
        first_prev = ((pid - 1) % tiles_per_seq) == 0
        csrc = [cbuf_r.at[s] for s in range(2)]

    def pool_sums():
        n2 = rows + HALO_C
        for s in range(2):
            s2buf[s, PAD_C:PAD_C + n2, :] = shifted(csrc[s], PAD_C, n2) + shifted(csrc[s], PAD_C - 1, n2)
        for s in range(2):
            s4buf[s, PAD_C:PAD_C + n2, :] = (
                shifted(s2buf.at[s], PAD_C, n2) + shifted(s2buf.at[s], PAD_C - 2, n2))
        n8 = rows + 8
        s8buf[c0 - 8:c0 - 8 + n8, :] = shifted(s4buf.at[1], c0 - 8, n8) + shifted(s4buf.at[1], c0 - 12, n8)

    def half(h):
        return slice(OUT_ROWS * h, OUT_ROWS * (h + 1))

    conv_a_tail = []

    def conv_a(h):
        for s in range(2):
            lo, hi = LANES * s, LANES * (s + 1)
            src = abuf_r.at[s]
            bias = jnp.broadcast_to(cab_ref[:, lo:hi], (SUBLANES, LANES))
            for g in range(OUT_ROWS // (SUBLANES * CONV_A_UNROLL)):
                g0 = OUT_ROWS * h + SUBLANES * CONV_A_UNROLL * g
                base = HALO_A + g0 - (CONV_A_K - 1)
                start = _after(bias, conv_a_tail[-1], slow=len(conv_a_tail) % 2 == 0) if conv_a_tail else bias
                accs = [start] * CONV_A_UNROLL
                for k in range(CONV_A_K):
                    wk = jnp.broadcast_to(caw_ref[k:k + 1, lo:hi], (SUBLANES, LANES))
                    for j in range(CONV_A_UNROLL):
                        accs[j] = accs[j] + shifted(src, base + SUBLANES * j + k, SUBLANES) * wk
                for j in range(CONV_A_UNROLL):
                    tbuf[g0 + SUBLANES * j:g0 + SUBLANES * (j + 1), lo:hi] = accs[j]
                conv_a_tail.append(accs[-1])

    def prep_cd(h):
        n, r0 = OUT_ROWS, OUT_ROWS * h
        low_half = jax.lax.broadcasted_iota(jnp.int32, (1, LANES), 1) < HEAD
        cr = c0 + r0
        s16 = shifted(s8buf, cr, n) + shifted(s8buf, cr - 8, n)
        inv2, inv4, inv8, inv16 = (1.0 / w for w in POOL_WINDOWS)
        mean0 = jnp.where(low_half, s2buf[0, cr:cr + n, :], s4buf[0, cr:cr + n, :]) * jnp.where(low_half, inv2, inv4)
        mean1 = jnp.where(low_half, s8buf[cr:cr + n, :], s16) * jnp.where(low_half, inv8, inv16)
        mean = jnp.concatenate([mean0, mean1], axis=-1)
        if h == 0:
            head_rows = mean[0:HALO_C, :] * jnp.where(first_prev, corr_ref[...], 1.0)
            mean = jnp.concatenate([head_rows, mean[HALO_C:, :]], axis=0)
        ch = jnp.concatenate([shifted(csrc[s], cr, n) for s in range(2)], axis=-1)
        pooled = (mean - ch).astype(BF16)
        v = _layer_norm_rows(kept_r[4, half(h), :], sg_ref[...], sb_ref[...]).astype(BF16)
        return pooled, v

    def mean_a(h):
        return _dot(tbuf[half(h), :].astype(BF16), phead_ref[...])

    def var_a(h, mu):
        d = tbuf[half(h), :] - mu
        tbuf[half(h), :] = d
        return _dot((d * d).astype(BF16), phead_ref[...])

    def dots_cd(pooled, v):
        yc = _dot(pooled, pbd_ref[...])
        head_of_lane = jax.lax.broadcasted_iota(jnp.int32, (1, GROUP), 1) // HEAD
        sp = []
        for q in range(OUT_ROWS // SGU_BLOCK):
            blk = v[SGU_BLOCK * q:SGU_BLOCK * (q + 1), :]
            vbig = jnp.concatenate(
                [jnp.where(head_of_lane == hd, blk, jnp.zeros_like(blk)) for hd in range(4)], axis=0)
            sp.append(_dot(ws_ref[...], vbig) + sbias_ref[...])
        return yc, jnp.concatenate(sp, axis=0)

    def finish(h, var, yc, sp):
        rs = half(h)
        r0, n = OUT_ROWS * h, OUT_ROWS
        half_an = tbuf[rs, :] * jax.lax.rsqrt(var + LN_EPS) * (0.5 * nag_ref[...]) + 0.5 * nab_ref[...]
        mix[rs, 0:GROUP] = (_silu_of_twice(half_an) * kept_r[0, rs, :]).astype(BF16)
        conv_b = []
        for s in range(2):
            lo, hi = LANES * s, LANES * (s + 1)
            src = bbuf_r.at[s]
            acc = shifted(src, HALO_B + r0, n) * cbw_ref[CONV_B_K - 1:CONV_B_K, lo:hi]
            for k in range(CONV_B_K - 1):
                acc = acc + shifted(src, HALO_B + r0 - (CONV_B_K - 1) + k, n) * cbw_ref[k:k + 1, lo:hi]
            conv_b.append(acc)
        mix[rs, GROUP:2 * GROUP] = (jnp.concatenate(conv_b, axis=-1) * kept_r[1, rs, :]).astype(BF16)
        mix[rs, 2 * GROUP:3 * GROUP] = (yc * kept_r[2, rs, :]).astype(BF16)
        mix[rs, 3 * GROUP:4 * GROUP] = (sp * kept_r[3, rs, :]).astype(BF16)

    def out_proj(h):
        o_ref[half(h), :] = _dot(mix[half(h), :], wout_ref[...]) + bout_ref[...]

    def post_norm(h):
        for q in range(OUT_ROWS * h, OUT_ROWS * (h + 1), NORM_ROWS):
            rs = slice(q, q + NORM_ROWS)
            r = ALPHA * xprev_ref[rs, :] + o_ref[rs, :]
            o_ref[rs, :] = _layer_norm_rows(r, lng_ref[...], lnb_ref[...])

    do1, do2 = cur_slot is not None, prev_slot is not None
    state = {}
    schedule = (
        (2, pool_sums), (2, lambda: conv_a(0)), (2, lambda: conv_a(1)),
        (1, proj_a), (1, proj_a_gate), (1, proj_b_gate), (1, proj_b), (1, proj_c),
        (2, lambda: state.update(cd0=prep_cd(0), cd1=prep_cd(1))),
        (2, lambda: state.update(mu0=mean_a(0), mu1=mean_a(1))),
        (1, proj_c_gate),
        (2, lambda: state.update(var0=var_a(0, state["mu0"]), var1=var_a(1, state["mu1"]))),
        (2, lambda: state.update(y0=dots_cd(*state["cd0"]), y1=dots_cd(*state["cd1"]))),
        (1, proj_d),
        (2, lambda: finish(0, state["var0"], *state["y0"])), (2, lambda: out_proj(0)),
        (2, lambda: finish(1, state["var1"], *state["y1"])), (2, lambda: out_proj(1)),
        (2, lambda: post_norm(0)),
        (1, proj_d_gate),
        (2, lambda: post_norm(1)),
    )
    for stage, piece in schedule:
        if (stage == 1 and do1) or (stage == 2 and do2):
            piece()

    if do1:
        if do2:
            fresh = (pid % tiles_per_seq) == 0
            abuf_w[:, 0:HALO_A, :] = jnp.where(fresh, 0.0, abuf_r[:, rows:rows + HALO_A, :])
            bbuf_w[:, 0:HALO_B, :] = jnp.where(fresh, 0.0, bbuf_r[:, rows:rows + HALO_B, :])
            cbuf_w[:, PAD_C:c0, :] = jnp.where(fresh, 0.0, cbuf_r[:, rows + PAD_C:rows + c0, :])
        else:
            abuf_w[:, 0:HALO_A, :] = jnp.zeros((2, HALO_A, LANES), F32)
            bbuf_w[:, 0:HALO_B, :] = jnp.zeros((2, HALO_B, LANES), F32)
            cbuf_w[:, PAD_C:c0, :] = jnp.zeros((2, HALO_C, LANES), F32)


def _layer_param_spec(layer, shape):
    return pl.BlockSpec((None,) + tuple(shape[1:]), lambda i: (layer,) + (0,) * (len(shape) - 1))


def _shared_spec(shape):
    return pl.BlockSpec(shape, lambda i: (0,) * len(shape))


def _layer(x2, seq_len, layer, stacked, shared):
    n = x2.shape[0]
    assert seq_len % TILE_ROWS == 0 and TILE_ROWS == 2 * OUT_ROWS and OUT_ROWS % SGU_BLOCK == 0
    assert len(stacked) == N_LAYER_PARAMS
    n_tiles = n // TILE_ROWS
    tile = (TILE_ROWS, D_MODEL)
    in_specs = [pl.BlockSpec(tile, lambda i: (jnp.minimum(i, n_tiles - 1), 0)),
                pl.BlockSpec(tile, lambda i: (jnp.maximum(i - 1, 0), 0))]
    in_specs += [_layer_param_spec(layer, p.shape) for p in stacked]
    in_specs += [_shared_spec(p.shape) for p in shared]
    slot = [
        pltpu.VMEM((2, HALO_A + TILE_ROWS, LANES), F32),
        pltpu.VMEM((2, HALO_B + TILE_ROWS, LANES), F32),
        pltpu.VMEM((2, PAD_C + HALO_C + TILE_ROWS, LANES), F32),
        pltpu.VMEM((N_KEPT, TILE_ROWS, GROUP), F32),
    ]
    assert len(slot) == N_SLOT_BUFFERS
    work = lambda *lead: pltpu.VMEM((*lead, PAD_C + HALO_C + TILE_ROWS, LANES), F32)
    return pl.pallas_call(
        functools.partial(_layer_kernel, tiles_per_seq=seq_len // TILE_ROWS, n_tiles=n_tiles),
        grid=(n_tiles + 1,),
        in_specs=in_specs,
        out_specs=pl.BlockSpec(tile, lambda i: (jnp.maximum(i - 1, 0), 0)),
        out_shape=jax.ShapeDtypeStruct((n, D_MODEL), F32),
        scratch_shapes=slot + slot + [
            work(2),
            work(2),
            work(),
            pltpu.VMEM((TILE_ROWS, GROUP), F32),
            pltpu.VMEM((TILE_ROWS, D_MODEL), BF16),
            pltpu.VMEM((TILE_ROWS, D_MODEL), BF16),
            pltpu.VMEM((SGU_BLOCK, 4 * SGU_BLOCK), BF16),
            pltpu.VMEM((D_MODEL, 12 * GROUP), BF16),
            pltpu.VMEM((D_MODEL, D_MODEL), BF16),
        ],
        compiler_params=pltpu.CompilerParams(
            dimension_semantics=("arbitrary",), vmem_limit_bytes=VMEM_LIMIT_BYTES),
        name="hybrid_layer",
    )(x2, x2, *stacked, *shared)


def _pack_small(b_in, conv_a_b, norm_a_g, norm_a_b, pool_scale, sgu_ln_g, sgu_ln_b, b_out, ln_g, ln_b,
                conv_b_w):
    depth, width = b_in.shape
    lanes = lambda v: jnp.pad(v, ((0, 0), (0, 0), (0, width - v.shape[-1])))
    row = lambda *vs: lanes(jnp.concatenate(vs, axis=-1)[:, None, :])
    blank = lambda n: jnp.zeros((depth, n, width), F32)
    return jnp.concatenate([
        row(b_in),
        row(conv_a_b, norm_a_g, norm_a_b, pool_scale, sgu_ln_g, sgu_ln_b),
        row(b_out, ln_g, ln_b),
        blank(CONV_B_ROW - 3),
        lanes(conv_b_w),
        blank(SMALL_ROWS - CONV_B_ROW - CONV_B_K),
    ], axis=1)


def _constants():
    head_id = np.arange(GROUP) // HEAD
    phead = (head_id[:, None] == head_id[None, :]).astype(np.float32) / HEAD
    win = np.asarray(POOL_WINDOWS, np.float32)[head_id]
    pos1 = np.arange(1, HALO_C + 1, dtype=np.float32)[:, None]
    corr = win[None, :] / np.minimum(pos1, win[None, :])
    return jnp.asarray(phead, BF16), jnp.asarray(corr, F32)


def kernel(x, ln_g, ln_b, w_in, b_in, conv_a_w, conv_a_b, norm_a_g, norm_a_b, conv_b_w, pool_w, pool_scale, sgu_ln_g, sgu_ln_b, sgu_w, sgu_bias, w_out, b_out):
    bsz, seq_len, d = x.shape
    depth = w_in.shape[0]
    phead, corr = _constants()
    n_heads = len(POOL_WINDOWS)
    same_head = jnp.eye(n_heads, dtype=F32)[None, :, None, :, None]
    pool_bd = (pool_w[:, :, :, None, :] * same_head).reshape(depth, GROUP, GROUP)
    sgu_wide = jnp.transpose(sgu_w, (0, 2, 1, 3)).reshape(depth, SGU_BLOCK, 4 * SGU_BLOCK)
    sgu_bias_tbl = jnp.repeat(jnp.transpose(sgu_bias, (0, 2, 1)), HEAD, axis=2)
    small = _pack_small(b_in, conv_a_b, norm_a_g, norm_a_b, pool_scale, sgu_ln_g, sgu_ln_b, b_out, ln_g, ln_b,
                        conv_b_w)
    stacked = (w_in, small, conv_a_w, pool_bd.astype(BF16), sgu_wide, sgu_bias_tbl, w_out)
    shared = (corr, phead)
    h = x.reshape(bsz * seq_len, d)
    for layer in range(depth):
        h = _layer(h, seq_len, layer, stacked, shared)
    return h.reshape(bsz, seq_len, d)
```

```python
import functools

import jax
import jax.numpy as jnp
import numpy as np
from jax.experimental import pallas as pl
from jax.experimental.pallas import tpu as pltpu

D_MODEL = 1024
DEPTH = 4
GROUP = 256
HEAD = 64
CONV_A_K = 31
CONV_B_K = 3
POOL_WINDOWS = (2, 4, 8, 16)
SGU_BLOCK = 128
CHUNK = 64
LN_EPS = 1e-5
ALPHA = float((2.0 * DEPTH) ** 0.25)

LANES = 128
SUBLANES = 8
TILE_ROWS = 512
OUT_ROWS = 256
HALO_A = 32
HALO_B = 32
HALO_C = 16
PAD_C = 16
N_KEPT = 5
GATE_SLICES = (1, 2, 6, 8, 11)
HALVED_SLICES = (0,) + GATE_SLICES
WEIGHT_CAST_ROWS = 128
CONV_A_UNROLL = 4
VMEM_LIMIT_BYTES = 56 * 1024 * 1024
N_LAYER_PARAMS = 7
SMALL_ROWS = 16
CONV_B_ROW = 8
N_SLOT_BUFFERS = 4

F32 = jnp.float32
BF16 = jnp.bfloat16


def _silu_of_twice(half_v):
    return half_v * jnp.tanh(half_v) + half_v


def _dot(a, b):
    return jnp.dot(a, b, preferred_element_type=F32)


def _after(value, anchor, slow=False):
    bits = jax.lax.bitcast_convert_type(anchor, jnp.uint32)
    zero = jax.lax.shift_right_logical(jax.lax.shift_right_logical(bits, jnp.uint32(16)), jnp.uint32(16))
    if slow:
        total = jnp.sum(jax.lax.bitcast_convert_type(zero, F32), axis=-1, keepdims=True)
        zero = jax.lax.bitcast_convert_type(jnp.broadcast_to(total, value.shape), jnp.uint32)
    return jax.lax.bitcast_convert_type(jax.lax.bitcast_convert_type(value, jnp.uint32) | zero, F32)


def _layer_norm_rows(v, g, b):
    mu = jnp.mean(v, axis=-1, keepdims=True)
    d = v - mu
    var = jnp.mean(d * d, axis=-1, keepdims=True)
    return d * jax.lax.rsqrt(var + LN_EPS) * g + b


def _layer_kernel(*refs, tiles_per_seq, n_tiles):
    x_ref, xprev_ref = refs[0:2]
    params = refs[2:2 + N_LAYER_PARAMS]
    corr_ref, phead_ref, prow_ref, o_ref = refs[2 + N_LAYER_PARAMS:6 + N_LAYER_PARAMS]
    scratch = refs[6 + N_LAYER_PARAMS:]
    slots = (scratch[0:N_SLOT_BUFFERS], scratch[N_SLOT_BUFFERS:2 * N_SLOT_BUFFERS])
    s2buf, s4buf, s8buf, tbuf, mix, xb_ref, ws_ref, win_bf, wout_bf = scratch[2 * N_SLOT_BUFFERS:]
    win_ref, swb_ref, wout_ref = params[0], params[4], params[6]
    pid = pl.program_id(0)

    step = functools.partial(
        _pipeline_step, pid, tiles_per_seq, x_ref, xprev_ref, params, corr_ref, phead_ref, prow_ref, o_ref,
        s2buf, s4buf, s8buf, tbuf, mix, xb_ref, ws_ref, win_bf, wout_bf)

    @pl.when(pid == 0)
    def _first():
        slice_of_lane = jax.lax.broadcasted_iota(jnp.int32, (1, 12 * GROUP), 1) // GROUP
        is_halved = functools.reduce(jnp.logical_or, [slice_of_lane == g for g in HALVED_SLICES])
        gate_scale = jnp.where(is_halved, 0.5, 1.0)

        def cast_rows(k, carry):
            rs = pl.ds(pl.multiple_of(k * WEIGHT_CAST_ROWS, WEIGHT_CAST_ROWS), WEIGHT_CAST_ROWS)
            win_bf[rs, :] = (win_ref[rs, :] * gate_scale).astype(BF16)
            wout_bf[rs, :] = wout_ref[rs, :].astype(BF16)
            return carry
        jax.lax.fori_loop(0, D_MODEL // WEIGHT_CAST_ROWS, cast_rows, 0)
        for slot in slots:
            slot[2][:, 0:PAD_C, :] = jnp.zeros((2, PAD_C, LANES), F32)
        s2buf[:, 0:PAD_C, :] = jnp.zeros((2, PAD_C, LANES), F32)
        wi = jax.lax.broadcasted_iota(jnp.int32, (SGU_BLOCK, 4 * SGU_BLOCK), 0)
        wj = jax.lax.broadcasted_iota(jnp.int32, (SGU_BLOCK, 4 * SGU_BLOCK), 1) % SGU_BLOCK
        ws_ref[...] = jnp.where(wj // CHUNK <= wi // CHUNK, swb_ref[...], 0.0).astype(BF16)
        step(slots[0], None)

    for parity in range(2):
        @pl.when((pid > 0) & (pid < n_tiles) & (pid % 2 == parity))
        def _middle(parity=parity):
            step(slots[parity], slots[1 - parity])

    @pl.when(pid == n_tiles)
    def _last():
        step(None, slots[(n_tiles - 1) % 2])


def _pipeline_step(pid, tiles_per_seq, x_ref, xprev_ref, params, corr_ref, phead_ref, prow_ref, o_ref,
                   s2buf, s4buf, s8buf, tbuf, mix, xb_ref, ws_ref, win_ref, wout_ref,
                   cur_slot, prev_slot):
    _, small_ref, caw_ref, pbd_ref, _, sbias_ref, _ = params
    bin_ref = small_ref.at[0:1, :]
    cab_ref, nag_ref, nab_ref, psc_ref, sg_ref, sb_ref = (
        small_ref.at[1:2, GROUP * i:GROUP * (i + 1)] for i in range(6))
    bout_ref, lng_ref, lnb_ref = (small_ref.at[2:3, D_MODEL * i:D_MODEL * (i + 1)] for i in range(3))
    cbw_ref = small_ref.at[CONV_B_ROW:CONV_B_ROW + CONV_B_K, 0:GROUP]
    rows = TILE_ROWS
    dz = jnp.minimum(pid, 0)
    c0 = PAD_C + HALO_C

    def shifted(ref, start, n):
        return ref[pl.ds(dz + start, n), :]

    if cur_slot is not None:
        abuf_w, bbuf_w, cbuf_w, kept_w = cur_slot
        xb_ref[...] = x_ref[...].astype(BF16)

    def proj(i):
        lo, hi = GROUP * i, GROUP * (i + 1)
        return _dot(xb_ref[...], win_ref[:, lo:hi]) + bin_ref[:, lo:hi]

    def proj_half(i):
        assert i in HALVED_SLICES
        lo, hi = GROUP * i, GROUP * (i + 1)
        return _dot(xb_ref[...], win_ref[:, lo:hi]) + 0.5 * bin_ref[:, lo:hi]

    def put_slabs(buf, halo, val):
        for s in range(2):
            buf[s, halo:halo + rows, :] = val[:, LANES * s:LANES * (s + 1)]

    def proj_a():
        half_val = proj_half(0)
        put_slabs(abuf_w, HALO_A, half_val * jnp.tanh(proj_half(1)) + half_val)

    def proj_a_gate():
        kept_w[0] = _silu_of_twice(proj_half(2))

    def proj_b_gate():
        kept_w[1] = proj(3) * _silu_of_twice(proj_half(6))

    def proj_b():
        put_slabs(bbuf_w, HALO_B, proj(4) * proj(5))

    def proj_c():
        put_slabs(cbuf_w, c0, proj(7))

    def proj_c_gate():
        kept_w[2] = _silu_of_twice(proj_half(8)) * psc_ref[...]

    def proj_d_gate():
        kept_w[3] = proj(9) * _silu_of_twice(proj_half(11))

    def proj_d():
        kept_w[4] = proj(10)

    if prev_slot is not None:
        abuf_r, bbuf_r, cbuf_r, kept_r = prev_slot
        first_prev = ((pid - 1) % tiles_per_seq) == 0
        csrc = [cbuf_r.at[s] for s in range(2)]

    def pool_sums():
        n2 = rows + HALO_C
        for s in range(2):
            s2buf[s, PAD_C:PAD_C + n2, :] = shifted(csrc[s], PAD_C, n2) + shifted(csrc[s], PAD_C - 1, n2)
        for s in range(2):
            s4buf[s, PAD_C:PAD_C + n2, :] = (
                shifted(s2buf.at[s], PAD_C, n2) + shifted(s2buf.at[s], PAD_C - 2, n2))
        n8 = rows + 8
        s8buf[c0 - 8:c0 - 8 + n8, :] = shifted(s4buf.at[1], c0 - 8, n8) + shifted(s4buf.at[1], c0 - 12, n8)

    def half(h):
        return slice(OUT_ROWS * h, OUT_ROWS * (h + 1))

    conv_a_tail = []

    def conv_a(h):
        for s in range(2):
            lo, hi = LANES * s, LANES * (s + 1)
            src = abuf_r.at[s]
            bias = jnp.broadcast_to(cab_ref[:, lo:hi], (SUBLANES, LANES))
            for g in range(OUT_ROWS // (SUBLANES * CONV_A_UNROLL)):
                g0 = OUT_ROWS * h + SUBLANES * CONV_A_UNROLL * g
                base = HALO_A + g0 - (CONV_A_K - 1)
                start = _after(bias, conv_a_tail[-1], slow=len(conv_a_tail) % 2 == 0) if conv_a_tail else bias
                accs = [start] * CONV_A_UNROLL
                for k in range(CONV_A_K):
                    wk = jnp.broadcast_to(caw_ref[k:k + 1, lo:hi], (SUBLANES, LANES))
                    for j in range(CONV_A_UNROLL):
                        accs[j] = accs[j] + shifted(src, base + SUBLANES * j + k, SUBLANES) * wk
                for j in range(CONV_A_UNROLL):
                    tbuf[g0 + SUBLANES * j:g0 + SUBLANES * (j + 1), lo:hi] = accs[j]
                conv_a_tail.append(accs[-1])

    def prep_cd(h):
        n, r0 = OUT_ROWS, OUT_ROWS * h
        low_half = jax.lax.broadcasted_iota(jnp.int32, (1, LANES), 1) < HEAD
        cr = c0 + r0
        s16 = shifted(s8buf, cr, n) + shifted(s8buf, cr - 8, n)
        inv2, inv4, inv8, inv16 = (1.0 / w for w in POOL_WINDOWS)
        mean0 = jnp.where(low_half, s2buf[0, cr:cr + n, :], s4buf[0, cr:cr + n, :]) * jnp.where(low_half, inv2, inv4)
        mean1 = jnp.where(low_half, s8buf[cr:cr + n, :], s16) * jnp.where(low_half, inv8, inv16)
        mean = jnp.concatenate([mean0, mean1], axis=-1)
        if h == 0:
            head_rows = mean[0:HALO_C, :] * jnp.where(first_prev, corr_ref[...], 1.0)
            mean = jnp.concatenate([head_rows, mean[HALO_C:, :]], axis=0)
        ch = jnp.concatenate([shifted(csrc[s], cr, n) for s in range(2)], axis=-1)
        pooled = (mean - ch).astype(BF16)
        v = _layer_norm_rows(kept_r[4, half(h), :], sg_ref[...], sb_ref[...]).astype(BF16)
        return pooled, v

    def mean_a(h):
        return _dot(tbuf[half(h), :].astype(BF16), phead_ref[...])

    def var_a(h, mu):
        d = tbuf[half(h), :] - mu
        tbuf[half(h), :] = d
        return _dot((d * d).astype(BF16), phead_ref[...])

    def dots_cd(pooled, v):
        yc = _dot(pooled, pbd_ref[...])
        head_of_lane = jax.lax.broadcasted_iota(jnp.int32, (1, GROUP), 1) // HEAD
        sp = []
        for q in range(OUT_ROWS // SGU_BLOCK):
            blk = v[SGU_BLOCK * q:SGU_BLOCK * (q + 1), :]
            vbig = jnp.concatenate(
                [jnp.where(head_of_lane == hd, blk, jnp.zeros_like(blk)) for hd in range(4)], axis=0)
            sp.append(_dot(ws_ref[...], vbig) + sbias_ref[...])
        return yc, jnp.concatenate(sp, axis=0)

    def finish(h, var, yc, sp):
        rs = half(h)
        r0, n = OUT_ROWS * h, OUT_ROWS
        half_an = tbuf[rs, :] * jax.lax.rsqrt(var + LN_EPS) * (0.5 * nag_ref[...]) + 0.5 * nab_ref[...]
        mix[rs, 0:GROUP] = (_silu_of_twice(half_an) * kept_r[0, rs, :]).astype(BF16)
        conv_b = []
        for s in range(2):
            lo, hi = LANES * s, LANES * (s + 1)
            src = bbuf_r.at[s]
            acc = shifted(src, HALO_B + r0, n) * cbw_ref[CONV_B_K - 1:CONV_B_K, lo:hi]
            for k in range(CONV_B_K - 1):
                acc = acc + shifted(src, HALO_B + r0 - (CONV_B_K - 1) + k, n) * cbw_ref[k:k + 1, lo:hi]
            conv_b.append(acc)
        mix[rs, GROUP:2 * GROUP] = (jnp.concatenate(conv_b, axis=-1) * kept_r[1, rs, :]).astype(BF16)
        mix[rs, 2 * GROUP:3 * GROUP] = (yc * kept_r[2, rs, :]).astype(BF16)
        mix[rs, 3 * GROUP:4 * GROUP] = (sp * kept_r[3, rs, :]).astype(BF16)

    def out_proj(h):
        o_ref[half(h), :] = _dot(mix[half(h), :], wout_ref[...]) + bout_ref[...]

    def post_norm(h):
        rs = half(h)
        wide = lambda stat: jnp.concatenate([stat] * (D_MODEL // LANES), axis=-1)
        r = ALPHA * xprev_ref[rs, :] + o_ref[rs, :]
        o_ref[rs, :] = r
        mu = _dot(r.astype(BF16), prow_ref[...])
        d = o_ref[rs, :] - wide(mu)
        o_ref[rs, :] = d
        var = _dot((d * d).astype(BF16), prow_ref[...])
        o_ref[rs, :] = o_ref[rs, :] * wide(jax.lax.rsqrt(var + LN_EPS)) * lng_ref[...] + lnb_ref[...]

    do1, do2 = cur_slot is not None, prev_slot is not None
    state = {}
    schedule = (
        (2, pool_sums), (2, lambda: conv_a(0)), (2, lambda: conv_a(1)),
        (1, proj_a), (1, proj_a_gate), (1, proj_b_gate), (1, proj_b), (1, proj_c),
        (2, lambda: state.update(cd0=prep_cd(0), cd1=prep_cd(1))),
        (2, lambda: state.update(mu0=mean_a(0), mu1=mean_a(1))),
        (1, proj_c_gate),
        (2, lambda: state.update(var0=var_a(0, state["mu0"]), var1=var_a(1, state["mu1"]))),
        (2, lambda: state.update(y0=dots_cd(*state["cd0"]), y1=dots_cd(*state["cd1"]))),
        (1, proj_d),
        (2, lambda: finish(0, state["var0"], *state["y0"])), (2, lambda: out_proj(0)),
        (2, lambda: finish(1, state["var1"], *state["y1"])), (2, lambda: out_proj(1)),
        (2, lambda: post_norm(0)),
        (1, proj_d_gate),
        (2, lambda: post_norm(1)),
    )
    for stage, piece in schedule:
        if (stage == 1 and do1) or (stage == 2 and do2):
            piece()

    if do1:
        if do2:
            fresh = (pid % tiles_per_seq) == 0
            abuf_w[:, 0:HALO_A, :] = jnp.where(fresh, 0.0, abuf_r[:, rows:rows + HALO_A, :])
            bbuf_w[:, 0:HALO_B, :] = jnp.where(fresh, 0.0, bbuf_r[:, rows:rows + HALO_B, :])
            cbuf_w[:, PAD_C:c0, :] = jnp.where(fresh, 0.0, cbuf_r[:, rows + PAD_C:rows + c0, :])
        else:
            abuf_w[:, 0:HALO_A, :] = jnp.zeros((2, HALO_A, LANES), F32)
            bbuf_w[:, 0:HALO_B, :] = jnp.zeros((2, HALO_B, LANES), F32)
            cbuf_w[:, PAD_C:c0, :] = jnp.zeros((2, HALO_C, LANES), F32)


def _layer_param_spec(layer, shape):
    return pl.BlockSpec((None,) + tuple(shape[1:]), lambda i: (layer,) + (0,) * (len(shape) - 1))


def _shared_spec(shape):
    return pl.BlockSpec(shape, lambda i: (0,) * len(shape))


def _layer(x2, seq_len, layer, stacked, shared):
    n = x2.shape[0]
    assert seq_len % TILE_ROWS == 0 and TILE_ROWS == 2 * OUT_ROWS and OUT_ROWS % SGU_BLOCK == 0
    assert len(stacked) == N_LAYER_PARAMS
    n_tiles = n // TILE_ROWS
    tile = (TILE_ROWS, D_MODEL)
    in_specs = [pl.BlockSpec(tile, lambda i: (jnp.minimum(i, n_tiles - 1), 0)),
                pl.BlockSpec(tile, lambda i: (jnp.maximum(i - 1, 0), 0))]
    in_specs += [_layer_param_spec(layer, p.shape) for p in stacked]
    in_specs += [_shared_spec(p.shape) for p in shared]
    slot = [
        pltpu.VMEM((2, HALO_A + TILE_ROWS, LANES), F32),
        pltpu.VMEM((2, HALO_B + TILE_ROWS, LANES), F32),
        pltpu.VMEM((2, PAD_C + HALO_C + TILE_ROWS, LANES), F32),
        pltpu.VMEM((N_KEPT, TILE_ROWS, GROUP), F32),
    ]
    assert len(slot) == N_SLOT_BUFFERS
    work = lambda *lead: pltpu.VMEM((*lead, PAD_C + HALO_C + TILE_ROWS, LANES), F32)
    return pl.pallas_call(
        functools.partial(_layer_kernel, tiles_per_seq=seq_len // TILE_ROWS, n_tiles=n_tiles),
        grid=(n_tiles + 1,),
        in_specs=in_specs,
        out_specs=pl.BlockSpec(tile, lambda i: (jnp.maximum(i - 1, 0), 0)),
        out_shape=jax.ShapeDtypeStruct((n, D_MODEL), F32),
        scratch_shapes=slot + slot + [
            work(2),
            work(2),
            work(),
            pltpu.VMEM((TILE_ROWS, GROUP), F32),
            pltpu.VMEM((TILE_ROWS, D_MODEL), BF16),
            pltpu.VMEM((TILE_ROWS, D_MODEL), BF16),
            pltpu.VMEM((SGU_BLOCK, 4 * SGU_BLOCK), BF16),
            pltpu.VMEM((D_MODEL, 12 * GROUP), BF16),
            pltpu.VMEM((D_MODEL, D_MODEL), BF16),
        ],
        compiler_params=pltpu.CompilerParams(
            dimension_semantics=("arbitrary",), vmem_limit_bytes=VMEM_LIMIT_BYTES),
        name="hybrid_layer",
    )(x2, x2, *stacked, *shared)


def _pack_small(b_in, conv_a_b, norm_a_g, norm_a_b, pool_scale, sgu_ln_g, sgu_ln_b, b_out, ln_g, ln_b,
                conv_b_w):
    depth, width = b_in.shape
    lanes = lambda v: jnp.pad(v, ((0, 0), (0, 0), (0, width - v.shape[-1])))
    row = lambda *vs: lanes(jnp.concatenate(vs, axis=-1)[:, None, :])
    blank = lambda n: jnp.zeros((depth, n, width), F32)
    return jnp.concatenate([
        row(b_in),
        row(conv_a_b, norm_a_g, norm_a_b, pool_scale, sgu_ln_g, sgu_ln_b),
        row(b_out, ln_g, ln_b),
        blank(CONV_B_ROW - 3),
        lanes(conv_b_w),
        blank(SMALL_ROWS - CONV_B_ROW - CONV_B_K),
    ], axis=1)


def _constants():
    head_id = np.arange(GROUP) // HEAD
    phead = (head_id[:, None] == head_id[None, :]).astype(np.float32) / HEAD
    win = np.asarray(POOL_WINDOWS, np.float32)[head_id]
    pos1 = np.arange(1, HALO_C + 1, dtype=np.float32)[:, None]
    corr = win[None, :] / np.minimum(pos1, win[None, :])
    prow = np.full((D_MODEL, LANES), 1.0 / D_MODEL, np.float32)
    return jnp.asarray(phead, BF16), jnp.asarray(prow, BF16), jnp.asarray(corr, F32)


def kernel(x, ln_g, ln_b, w_in, b_in, conv_a_w, conv_a_b, norm_a_g, norm_a_b, conv_b_w, pool_w, pool_scale, sgu_ln_g, sgu_ln_b, sgu_w, sgu_bias, w_out, b_out):
    bsz, seq_len, d = x.shape
    depth = w_in.shape[0]
    phead, prow, corr = _constants()
    n_heads = len(POOL_WINDOWS)
    same_head = jnp.eye(n_heads, dtype=F32)[None, :, None, :, None]
    pool_bd = (pool_w[:, :, :, None, :] * same_head).reshape(depth, GROUP, GROUP)
    sgu_wide = jnp.transpose(sgu_w, (0, 2, 1, 3)).reshape(depth, SGU_BLOCK, 4 * SGU_BLOCK)
    sgu_bias_tbl = jnp.repeat(jnp.transpose(sgu_bias, (0, 2, 1)), HEAD, axis=2)
    small = _pack_small(b_in, conv_a_b, norm_a_g, norm_a_b, pool_scale, sgu_ln_g, sgu_ln_b, b_out, ln_g, ln_b,
                        conv_b_w)
    stacked = (w_in, small, conv_a_w, pool_bd.astype(BF16), sgu_wide, sgu_bias_tbl, w_out)
    shared = (corr, phead, prow)
    h = x.reshape(bsz * seq_len, d)
    for layer in range(depth):
        h = _layer(h, seq_len, layer, stacked, shared)
    return h.reshape(bsz, seq_len, d)
```

```python
import functools

import jax
import jax.numpy as jnp
import numpy as np
from jax.experimental import pallas as pl
from jax.experimental.pallas import tpu as pltpu

D_MODEL = 1024
DEPTH = 4
GROUP = 256
HEAD = 64
CONV_A_K = 31
CONV_B_K = 3
POOL_WINDOWS = (2, 4, 8, 16)
SGU_BLOCK = 128
CHUNK = 64
LN_EPS = 1e-5
ALPHA = float((2.0 * DEPTH) ** 0.25)

LANES = 128
SUBLANES = 8
TILE_ROWS = 512
NORM_ROWS = SGU_BLOCK
OUT_ROWS = 256
HALO_A = 32
HALO_B = 32
HALO_C = 16
PAD_C = 16
N_KEPT = 5
GATE_SLICES = (1, 2, 6, 8, 11)
HALVED_SLICES = (0,) + GATE_SLICES
WEIGHT_CAST_ROWS = 128
CONV_A_UNROLL = 4
VMEM_LIMIT_BYTES = 56 * 1024 * 1024
N_LAYER_PARAMS = 7
SMALL_ROWS = 16
CONV_B_ROW = 8
N_SLOT_BUFFERS = 4

F32 = jnp.float32
BF16 = jnp.bfloat16


def _silu_of_twice(half_v):
    return half_v * jnp.tanh(half_v) + half_v


def _dot(a, b):
    return jnp.dot(a, b, preferred_element_type=F32)


def _after(value, anchor, slow=False):
    bits = jax.lax.bitcast_convert_type(anchor, jnp.uint32)
    zero = jax.lax.shift_right_logical(jax.lax.shift_right_logical(bits, jnp.uint32(16)), jnp.uint32(16))
    if slow:
        total = jnp.sum(jax.lax.bitcast_convert_type(zero, F32), axis=-1, keepdims=True)
        zero = jax.lax.bitcast_convert_type(jnp.broadcast_to(total, value.shape), jnp.uint32)
    return jax.lax.bitcast_convert_type(jax.lax.bitcast_convert_type(value, jnp.uint32) | zero, F32)


def _layer_norm_rows(v, g, b):
    mu = jnp.mean(v, axis=-1, keepdims=True)
    d = v - mu
    var = jnp.mean(d * d, axis=-1, keepdims=True)
    return d * jax.lax.rsqrt(var + LN_EPS) * g + b


def _layer_kernel(*refs, tiles_per_seq, n_tiles):
    x_ref, xprev_ref = refs[0:2]
    params = refs[2:2 + N_LAYER_PARAMS]
    corr_ref, phead_ref, o_ref = refs[2 + N_LAYER_PARAMS:5 + N_LAYER_PARAMS]
    scratch = refs[5 + N_LAYER_PARAMS:]
    slots = (scratch[0:N_SLOT_BUFFERS], scratch[N_SLOT_BUFFERS:2 * N_SLOT_BUFFERS])
    s2buf, s4buf, s8buf, tbuf, mix, xb_ref, ws_ref, win_bf, wout_bf = scratch[2 * N_SLOT_BUFFERS:]
    win_ref, swb_ref, wout_ref = params[0], params[4], params[6]
    pid = pl.program_id(0)

    step = functools.partial(
        _pipeline_step, pid, tiles_per_seq, x_ref, xprev_ref, params, corr_ref, phead_ref, o_ref,
        s2buf, s4buf, s8buf, tbuf, mix, xb_ref, ws_ref, win_bf, wout_bf)

    @pl.when(pid == 0)
    def _first():
        slice_of_lane = jax.lax.broadcasted_iota(jnp.int32, (1, 12 * GROUP), 1) // GROUP
        is_halved = functools.reduce(jnp.logical_or, [slice_of_lane == g for g in HALVED_SLICES])
        gate_scale = jnp.where(is_halved, 0.5, 1.0)

        def cast_rows(k, carry):
            rs = pl.ds(pl.multiple_of(k * WEIGHT_CAST_ROWS, WEIGHT_CAST_ROWS), WEIGHT_CAST_ROWS)
            win_bf[rs, :] = (win_ref[rs, :] * gate_scale).astype(BF16)
            wout_bf[rs, :] = wout_ref[rs, :].astype(BF16)
            return carry
        jax.lax.fori_loop(0, D_MODEL // WEIGHT_CAST_ROWS, cast_rows, 0)
        for slot in slots:
            slot[2][:, 0:PAD_C, :] = jnp.zeros((2, PAD_C, LANES), F32)
        s2buf[:, 0:PAD_C, :] = jnp.zeros((2, PAD_C, LANES), F32)
        wi = jax.lax.broadcasted_iota(jnp.int32, (SGU_BLOCK, 4 * SGU_BLOCK), 0)
        wj = jax.lax.broadcasted_iota(jnp.int32, (SGU_BLOCK, 4 * SGU_BLOCK), 1) % SGU_BLOCK
        ws_ref[...] = jnp.where(wj // CHUNK <= wi // CHUNK, swb_ref[...], 0.0).astype(BF16)
        step(slots[0], None)

    for parity in range(2):
        @pl.when((pid > 0) & (pid < n_tiles) & (pid % 2 == parity))
        def _middle(parity=parity):
            step(slots[parity], slots[1 - parity])

    @pl.when(pid == n_tiles)
    def _last():
        step(None, slots[(n_tiles - 1) % 2])


def _pipeline_step(pid, tiles_per_seq, x_ref, xprev_ref, params, corr_ref, phead_ref, o_ref,
                   s2buf, s4buf, s8buf, tbuf, mix, xb_ref, ws_ref, win_ref, wout_ref,
                   cur_slot, prev_slot):
    _, small_ref, caw_ref, pbd_ref, _, sbias_ref, _ = params
    bin_ref = small_ref.at[0:1, :]
    cab_ref, nag_ref, nab_ref, psc_ref, sg_ref, sb_ref = (
        small_ref.at[1:2, GROUP * i:GROUP * (i + 1)] for i in range(6))
    bout_ref, lng_ref, lnb_ref = (small_ref.at[2:3, D_MODEL * i:D_MODEL * (i + 1)] for i in range(3))
    cbw_ref = small_ref.at[CONV_B_ROW:CONV_B_ROW + CONV_B_K, 0:GROUP]
    rows = TILE_ROWS
    dz = jnp.minimum(pid, 0)
    c0 = PAD_C + HALO_C

    def shifted(ref, start, n):
        return ref[pl.ds(dz + start, n), :]

    if cur_slot is not None:
        abuf_w, bbuf_w, cbuf_w, kept_w = cur_slot
        xb_ref[...] = x_ref[...].astype(BF16)

    def proj(i):
        lo, hi = GROUP * i, GROUP * (i + 1)
        return _dot(xb_ref[...], win_ref[:, lo:hi]) + bin_ref[:, lo:hi]

    def proj_half(i):
        assert i in HALVED_SLICES
        lo, hi = GROUP * i, GROUP * (i + 1)
        return _dot(xb_ref[...], win_ref[:, lo:hi]) + 0.5 * bin_ref[:, lo:hi]

    def put_slabs(buf, halo, val):
        for s in range(2):
            buf[s, halo:halo + rows, :] = val[:, LANES * s:LANES * (s + 1)]

    def proj_a():
        half_val = proj_half(0)
        put_slabs(abuf_w, HALO_A, half_val * jnp.tanh(proj_half(1)) + half_val)

    def proj_a_gate():
        kept_w[0] = _silu_of_twice(proj_half(2))

    def proj_b_gate():
        kept_w[1] = proj(3) * _silu_of_twice(proj_half(6))

    def proj_b():
        put_slabs(bbuf_w, HALO_B, proj(4) * proj(5))

    def proj_c():
        put_slabs(cbuf_w, c0, proj(7))

    def proj_c_gate():
        kept_w[2] = _silu_of_twice(proj_half(8)) * psc_ref[...]

    def proj_d_gate():
        kept_w[3] = proj(9) * _silu_of_twice(proj_half(11))

    def proj_d():
        kept_w[4] = proj(10)

    if prev_slot is not None:
        abuf_r, bbuf_r, cbuf_r, kept_r = prev_slot
        first_prev = ((pid - 1) % tiles_per_seq) == 0
        csrc = [cbuf_r.at[s] for s in range(2)]

    def pool_sums():
        n2 = rows + HALO_C
        for s in range(2):
            s2buf[s, PAD_C:PAD_C + n2, :] = shifted(csrc[s], PAD_C, n2) + shifted(csrc[s], PAD_C - 1, n2)
        for s in range(2):
            s4buf[s, PAD_C:PAD_C + n2, :] = (
                shifted(s2buf.at[s], PAD_C, n2) + shifted(s2buf.at[s], PAD_C - 2, n2))
        n8 = rows + 8
        s8buf[c0 - 8:c0 - 8 + n8, :] = shifted(s4buf.at[1], c0 - 8, n8) + shifted(s4buf.at[1], c0 - 12, n8)

    def half(h):
        return slice(OUT_ROWS * h, OUT_ROWS * (h + 1))

    conv_a_tail = []

    def conv_a(h):
        for s in range(2):
            lo, hi = LANES * s, LANES * (s + 1)
            src = abuf_r.at[s]
            bias = jnp.broadcast_to(cab_ref[:, lo:hi], (SUBLANES, LANES))
            for g in range(OUT_ROWS // (SUBLANES * CONV_A_UNROLL)):
                g0 = OUT_ROWS * h + SUBLANES * CONV_A_UNROLL * g
                base = HALO_A + g0 - (CONV_A_K - 1)
                start = _after(bias, conv_a_tail[-1], slow=len(conv_a_tail) % 2 == 0) if conv_a_tail else bias
                accs = [start] * CONV_A_UNROLL
                for k in range(CONV_A_K):
                    wk = jnp.broadcast_to(caw_ref[k:k + 1, lo:hi], (SUBLANES, LANES))
                    for j in range(CONV_A_UNROLL):
                        accs[j] = accs[j] + shifted(src, base + SUBLANES * j + k, SUBLANES) * wk
                for j in range(CONV_A_UNROLL):
                    tbuf[g0 + SUBLANES * j:g0 + SUBLANES * (j + 1), lo:hi] = accs[j]
                conv_a_tail.append(accs[-1])

    def prep_cd(h):
        n, r0 = OUT_ROWS, OUT_ROWS * h
        low_half = jax.lax.broadcasted_iota(jnp.int32, (1, LANES), 1) < HEAD
        cr = c0 + r0
        s16 = shifted(s8buf, cr, n) + shifted(s8buf, cr - 8, n)
        inv2, inv4, inv8, inv16 = (1.0 / w for w in POOL_WINDOWS)
        mean0 = jnp.where(low_half, s2buf[0, cr:cr + n, :], s4buf[0, cr:cr + n, :]) * jnp.where(low_half, inv2, inv4)
        mean1 = jnp.where(low_half, s8buf[cr:cr + n, :], s16) * jnp.where(low_half, inv8, inv16)
        mean = jnp.concatenate([mean0, mean1], axis=-1)
        if h == 0:
            head_rows = mean[0:HALO_C, :] * jnp.where(first_prev, corr_ref[...], 1.0)
            mean = jnp.concatenate([head_rows, mean[HALO_C:, :]], axis=0)
        ch = jnp.concatenate([shifted(csrc[s], cr, n) for s in range(2)], axis=-1)
        pooled = (mean - ch).astype(BF16)
        v = _layer_norm_rows(kept_r[4, half(h), :], sg_ref[...], sb_ref[...]).astype(BF16)
        return pooled, v

    def mean_a(h):
        return _dot(tbuf[half(h), :].astype(BF16), phead_ref[...])

    def var_a(h, mu):
        d = tbuf[half(h), :] - mu
        tbuf[half(h), :] = d
        return _dot((d * d).astype(BF16), phead_ref[...])

    def dots_cd(pooled, v):
        yc = _dot(pooled, pbd_ref[...])
        head_of_lane = jax.lax.broadcasted_iota(jnp.int32, (1, GROUP), 1) // HEAD
        sp = []
        for q in range(OUT_ROWS // SGU_BLOCK):
            blk = v[SGU_BLOCK * q:SGU_BLOCK * (q + 1), :]
            vbig = jnp.concatenate(
                [jnp.where(head_of_lane == hd, blk, jnp.zeros_like(blk)) for hd in range(4)], axis=0)
            sp.append(_dot(ws_ref[...], vbig) + sbias_ref[...])
        return yc, jnp.concatenate(sp, axis=0)

    def finish(h, var, yc, sp):
        rs = half(h)
        r0, n = OUT_ROWS * h, OUT_ROWS
        half_an = tbuf[rs, :] * jax.lax.rsqrt(var + LN_EPS) * (0.5 * nag_ref[...]) + 0.5 * nab_ref[...]
        mix[rs, 0:GROUP] = (_silu_of_twice(half_an) * kept_r[0, rs, :]).astype(BF16)
        conv_b = []
        for s in range(2):
            lo, hi = LANES * s, LANES * (s + 1)
            src = bbuf_r.at[s]
            acc = shifted(src, HALO_B + r0, n) * cbw_ref[CONV_B_K - 1:CONV_B_K, lo:hi]
            for k in range(CONV_B_K - 1):
                acc = acc + shifted(src, HALO_B + r0 - (CONV_B_K - 1) + k, n) * cbw_ref[k:k + 1, lo:hi]
            conv_b.append(acc)
        mix[rs, GROUP:2 * GROUP] = (jnp.concatenate(conv_b, axis=-1) * kept_r[1, rs, :]).astype(BF16)
        mix[rs, 2 * GROUP:3 * GROUP] = (yc * kept_r[2, rs, :]).astype(BF16)
        mix[rs, 3 * GROUP:4 * GROUP] = (sp * kept_r[3, rs, :]).astype(BF16)

    def out_proj(h):
        o_ref[half(h), :] = _dot(mix[half(h), :], wout_ref[...]) + bout_ref[...]

    def post_norm(h):
        for q in range(OUT_ROWS * h, OUT_ROWS * (h + 1), NORM_ROWS):
            rs = slice(q, q + NORM_ROWS)
            r = ALPHA * xprev_ref[rs, :] + o_ref[rs, :]
            o_ref[rs, :] = _layer_norm_rows(r, lng_ref[...], lnb_ref[...])

    do1, do2 = cur_slot is not None, prev_slot is not None
    state = {}
    schedule = (
        (2, pool_sums), (2, lambda: conv_a(0)), (2, lambda: conv_a(1)),
        (1, proj_a), (1, proj_a_gate), (1, proj_b_gate), (1, proj_b), (1, proj_c),
        (2, lambda: state.update(cd0=prep_cd(0), cd1=prep_cd(1))),
        (2, lambda: state.update(mu0=mean_a(0), mu1=mean_a(1))),
        (1, proj_c_gate),
        (2, lambda: state.update(var0=var_a(0, state["mu0"]), var1=var_a(1, state["mu1"]))),
        (2, lambda: state.update(y0=dots_cd(*state["cd0"]), y1=dots_cd(*state["cd1"]))),
        (1, proj_d),
        (2, lambda: finish(0, state["var0"], *state["y0"])), (2, lambda: out_proj(0)),
        (2, lambda: finish(1, state["var1"], *state["y1"])), (2, lambda: out_proj(1)),
        (2, lambda: post_norm(0)),
        (1, proj_d_gate),
        (2, lambda: post_norm(1)),
    )
    for stage, piece in schedule:
        if (stage == 1 and do1) or (stage == 2 and do2):
            piece()

    if do1:
        if do2:
            fresh = (pid % tiles_per_seq) == 0
            abuf_w[:, 0:HALO_A, :] = jnp.where(fresh, 0.0, abuf_r[:, rows:rows + HALO_A, :])
            bbuf_w[:, 0:HALO_B, :] = jnp.where(fresh, 0.0, bbuf_r[:, rows:rows + HALO_B, :])
            cbuf_w[:, PAD_C:c0, :] = jnp.where(fresh, 0.0, cbuf_r[:, rows + PAD_C:rows + c0, :])
        else:
            abuf_w[:, 0:HALO_A, :] = jnp.zeros((2, HALO_A, LANES), F32)
            bbuf_w[:, 0:HALO_B, :] = jnp.zeros((2, HALO_B, LANES), F32)
            cbuf_w[:, PAD_C:c0, :] = jnp.zeros((2, HALO_C, LANES), F32)


def _layer_param_spec(layer, shape):
    return pl.BlockSpec((None,) + tuple(shape[1:]), lambda i: (layer,) + (0,) * (len(shape) - 1),
                        pipeline_mode=pl.Buffered(1))


def _shared_spec(shape):
    return pl.BlockSpec(shape, lambda i: (0,) * len(shape), pipeline_mode=pl.Buffered(1))


def _layer(x2, seq_len, layer, stacked, shared):
    n = x2.shape[0]
    assert seq_len % TILE_ROWS == 0 and TILE_ROWS == 2 * OUT_ROWS and OUT_ROWS % SGU_BLOCK == 0
    assert len(stacked) == N_LAYER_PARAMS
    n_tiles = n // TILE_ROWS
    tile = (TILE_ROWS, D_MODEL)
    in_specs = [pl.BlockSpec(tile, lambda i: (jnp.minimum(i, n_tiles - 1), 0)),
                pl.BlockSpec(tile, lambda i: (jnp.maximum(i - 1, 0), 0))]
    in_specs += [_layer_param_spec(layer, p.shape) for p in stacked]
    in_specs += [_shared_spec(p.shape) for p in shared]
    slot = [
        pltpu.VMEM((2, HALO_A + TILE_ROWS, LANES), F32),
        pltpu.VMEM((2, HALO_B + TILE_ROWS, LANES), F32),
        pltpu.VMEM((2, PAD_C + HALO_C + TILE_ROWS, LANES), F32),
        pltpu.VMEM((N_KEPT, TILE_ROWS, GROUP), F32),
    ]
    assert len(slot) == N_SLOT_BUFFERS
    work = lambda *lead: pltpu.VMEM((*lead, PAD_C + HALO_C + TILE_ROWS, LANES), F32)
    return pl.pallas_call(
        functools.partial(_layer_kernel, tiles_per_seq=seq_len // TILE_ROWS, n_tiles=n_tiles),
        grid=(n_tiles + 1,),
        in_specs=in_specs,
        out_specs=pl.BlockSpec(tile, lambda i: (jnp.maximum(i - 1, 0), 0)),
        out_shape=jax.ShapeDtypeStruct((n, D_MODEL), F32),
        scratch_shapes=slot + slot + [
            work(2),
            work(2),
            work(),
            pltpu.VMEM((TILE_ROWS, GROUP), F32),
            pltpu.VMEM((TILE_ROWS, D_MODEL), BF16),
            pltpu.VMEM((TILE_ROWS, D_MODEL), BF16),
            pltpu.VMEM((SGU_BLOCK, 4 * SGU_BLOCK), BF16),
            pltpu.VMEM((D_MODEL, 12 * GROUP), BF16),
            pltpu.VMEM((D_MODEL, D_MODEL), BF16),
        ],
        compiler_params=pltpu.CompilerParams(
            dimension_semantics=("arbitrary",), vmem_limit_bytes=VMEM_LIMIT_BYTES),
        name="hybrid_layer",
    )(x2, x2, *stacked, *shared)


def _pack_small(b_in, conv_a_b, norm_a_g, norm_a_b, pool_scale, sgu_ln_g, sgu_ln_b, b_out, ln_g, ln_b,
                conv_b_w):
    depth, width = b_in.shape
    lanes = lambda v: jnp.pad(v, ((0, 0), (0, 0), (0, width - v.shape[-1])))
    row = lambda *vs: lanes(jnp.concatenate(vs, axis=-1)[:, None, :])
    blank = lambda n: jnp.zeros((depth, n, width), F32)
    return jnp.concatenate([
        row(b_in),
        row(conv_a_b, norm_a_g, norm_a_b, pool_scale, sgu_ln_g, sgu_ln_b),
        row(b_out, ln_g, ln_b),
        blank(CONV_B_ROW - 3),
        lanes(conv_b_w),
        blank(SMALL_ROWS - CONV_B_ROW - CONV_B_K),
    ], axis=1)


def _constants():
    head_id = np.arange(GROUP) // HEAD
    phead = (head_id[:, None] == head_id[None, :]).astype(np.float32) / HEAD
    win = np.asarray(POOL_WINDOWS, np.float32)[head_id]
    pos1 = np.arange(1, HALO_C + 1, dtype=np.float32)[:, None]
    corr = win[None, :] / np.minimum(pos1, win[None, :])
    return jnp.asarray(phead, BF16), jnp.asarray(corr, F32)


def kernel(x, ln_g, ln_b, w_in, b_in, conv_a_w, conv_a_b, norm_a_g, norm_a_b, conv_b_w, pool_w, pool_scale, sgu_ln_g, sgu_ln_b, sgu_w, sgu_bias, w_out, b_out):
    bsz, seq_len, d = x.shape
    depth = w_in.shape[0]
    phead, corr = _constants()
    n_heads = len(POOL_WINDOWS)
    same_head = jnp.eye(n_heads, dtype=F32)[None, :, None, :, None]
    pool_bd = (pool_w[:, :, :, None, :] * same_head).reshape(depth, GROUP, GROUP)
    sgu_wide = jnp.transpose(sgu_w, (0, 2, 1, 3)).reshape(depth, SGU_BLOCK, 4 * SGU_BLOCK)
    sgu_bias_tbl = jnp.repeat(jnp.transpose(sgu_bias, (0, 2, 1)), HEAD, axis=2)
    small = _pack_small(b_in, conv_a_b, norm_a_g, norm_a_b, pool_scale, sgu_ln_g, sgu_ln_b, b_out, ln_g, ln_b,
                        conv_b_w)
    stacked = (w_in, small, conv_a_w, pool_bd.astype(BF16), sgu_wide, sgu_bias_tbl, w_out)
    shared = (corr, phead)
    h = x.reshape(bsz * seq_len, d)
    for layer in range(depth):
        h = _layer(h, seq_len, layer, stacked, shared)
    return h.reshape(bsz, seq_len, d)
```
